```python
import math
import jax, jax.numpy as jnp
from jax import lax
import numpy as np

D_MODEL = 1024
BATCH = 8
SEQ = 4096
DEPTH = 2
DEC_BATCH = 16
DEC_SEQ = 4096
PAST_LEN = 128

HEAD_DIM = 64
A_Q_HEADS = 4
A_KV_HEADS = 2
A_GROUP = A_Q_HEADS // A_KV_HEADS
A_HALF_WINDOW = 128
A_BLOCK = 128
D_A = A_Q_HEADS * HEAD_DIM
D_A_KV = A_KV_HEADS * HEAD_DIM
B_HEADS = 6
B_PATTERNS = ((128, 1), (512, 4), (2048, 16))
B_BLOCK = 64
D_B = B_HEADS * HEAD_DIM
C_CH = 384
C_CONV_WIDTH = 31
D_MIX = D_A + D_B + C_CH
IN_WIDTH = D_A + 2 * D_A_KV + 3 * D_B + 2 * C_CH
FF = 2816
FF_CONV_WIDTH = 3
ROPE_THETA = 10000.0
EPS = 1e-6
NEG = -1e30

kernel_name = "hymba_style_hybrid_encoder_two_batches"


def rmsnorm(x, g):
    xf = x.astype(jnp.float32)
    y = xf * lax.rsqrt(jnp.mean(xf * xf, axis=-1, keepdims=True) + EPS)
    return (y * g.astype(jnp.float32)).astype(x.dtype)


def layernorm(x, g, b):
    xf = x.astype(jnp.float32)
    mu = jnp.mean(xf, axis=-1, keepdims=True)
    var = jnp.mean(jnp.square(xf - mu), axis=-1, keepdims=True)
    y = (xf - mu) * lax.rsqrt(var + EPS)
    return (y * g.astype(jnp.float32) + b.astype(jnp.float32)).astype(x.dtype)


def rope(x):
    S, dh = x.shape[1], x.shape[-1]
    half = dh // 2
    inv = 1.0 / (jnp.float32(ROPE_THETA) ** (jnp.arange(half, dtype=jnp.float32) / half))
    ang = jnp.arange(S, dtype=jnp.float32)[:, None] * inv[None, :]
    cos = jnp.cos(ang)[None, :, None, :]
    sin = jnp.sin(ang)[None, :, None, :]
    xf = x.astype(jnp.float32)
    x1, x2 = xf[..., :half], xf[..., half:]
    return jnp.concatenate([x1 * cos - x2 * sin, x2 * cos + x1 * sin], axis=-1).astype(x.dtype)


def dwconv(x, w, b):
    k = w.shape[0]
    y = lax.conv_general_dilated(x, w[:, None, :].astype(x.dtype), window_strides=(1,),
                                 padding=[((k - 1) // 2, (k - 1) // 2)],
                                 dimension_numbers=('NWC', 'WIO', 'NWC'),
                                 feature_group_count=x.shape[-1])
    return y + b.astype(x.dtype)


def banded_attention(q, k, v, half, blk, sink=None):
    N, L, Hk, G, Dh = q.shape
    nb = -(-L // blk)
    Lp = nb * blk
    qp = jnp.pad(q, ((0, 0), (0, Lp - L), (0, 0), (0, 0), (0, 0)))
    kv_pad = ((0, 0), (blk, Lp - L + blk), (0, 0), (0, 0))
    kp = jnp.pad(k, kv_pad).reshape(N, nb + 2, blk, Hk, Dh)
    vp = jnp.pad(v, kv_pad).reshape(N, nb + 2, blk, Hk, Dh)
    kb = jnp.concatenate([kp[:, :-2], kp[:, 1:-1], kp[:, 2:]], axis=2)
    vb = jnp.concatenate([vp[:, :-2], vp[:, 1:-1], vp[:, 2:]], axis=2)
    qb = qp.reshape(N, nb, blk, Hk, G, Dh)
    s = jnp.einsum('nbqhgd,nbkhd->nbhgqk', qb, kb,
                   preferred_element_type=jnp.float32) * (1.0 / math.sqrt(Dh))
    qpos = jnp.arange(nb)[:, None] * blk + jnp.arange(blk)[None, :]
    kpos = jnp.arange(nb)[:, None] * blk - blk + jnp.arange(3 * blk)[None, :]
    valid = ((jnp.abs(kpos[:, None, :] - qpos[:, :, None]) <= half)
             & (kpos >= 0)[:, None, :] & (kpos < L)[:, None, :])
    s = jnp.where(valid[None, :, None, None], s, NEG)
    m = jnp.max(s, axis=-1, keepdims=True)
    if sink is not None:
        sk = sink.astype(jnp.float32)[None, None, :, :, None, None]
        m = jnp.maximum(m, sk)
    p = jnp.exp(s - m)
    denom = jnp.sum(p, axis=-1, keepdims=True)
    if sink is not None:
        denom = denom + jnp.exp(sk - m)
    o = jnp.einsum('nbhgqk,nbkhd->nbqhgd', p / denom, vb.astype(jnp.float32))
    lse = (jnp.log(denom) + m)[..., 0]
    lse = jnp.transpose(lse, (0, 1, 4, 2, 3)).reshape(N, Lp, Hk, G)[:, :L]
    o = o.reshape(N, Lp, Hk, G, Dh)[:, :L]
    return o.astype(q.dtype), lse


def head_rmsnorm(t, n_heads, g):
    N, S, _ = t.shape
    return rmsnorm(t.reshape(N, S, n_heads, HEAD_DIM), g)


def token_mixers(h, w_in, qn_a, kn_a, sink_a, qn_b, kn_b,
                 conv_c_w, conv_c_b, ln_c_g, ln_c_b, w_out):
    N, S, _ = h.shape
    proj = h @ w_in
    sizes = [D_A, D_A_KV, D_A_KV, D_B, D_B, D_B, C_CH, C_CH]
    points = [int(p) for p in np.cumsum(sizes)[:-1]]
    qa, ka, va, qb, kb, vb, ca, cg = jnp.split(proj, points, axis=-1)

    qa = rope(head_rmsnorm(qa, A_Q_HEADS, qn_a)).reshape(N, S, A_KV_HEADS, A_GROUP, HEAD_DIM)
    ka = rope(head_rmsnorm(ka, A_KV_HEADS, kn_a))
    va = va.reshape(N, S, A_KV_HEADS, HEAD_DIM)
    oa, _ = banded_attention(qa, ka, va, A_HALF_WINDOW, A_BLOCK,
                             sink_a.reshape(A_KV_HEADS, A_GROUP))
    oa = oa.reshape(N, S, D_A)

    qb = rope(head_rmsnorm(qb, B_HEADS, qn_b))
    kb = rope(head_rmsnorm(kb, B_HEADS, kn_b))
    vb = vb.reshape(N, S, B_HEADS, HEAD_DIM)
    outs, lses = [], []
    for window, dil in B_PATTERNS:
        half = window // 2 // dil
        Ls = S // dil

        def to_strided(t):
            return jnp.transpose(t.reshape(N, Ls, dil, B_HEADS, HEAD_DIM),
                                 (0, 2, 1, 3, 4)).reshape(N * dil, Ls, B_HEADS, HEAD_DIM)

        o, lse = banded_attention(to_strided(qb)[:, :, :, None, :], to_strided(kb),
                                  to_strided(vb), half, B_BLOCK)
        o = jnp.transpose(o.reshape(N, dil, Ls, B_HEADS, HEAD_DIM), (0, 2, 1, 3, 4)).reshape(N, S, B_HEADS, HEAD_DIM)
        lse = jnp.transpose(lse.reshape(N, dil, Ls, B_HEADS), (0, 2, 1, 3)).reshape(N, S, B_HEADS)
        outs.append(o)
        lses.append(lse)
    wts = jax.nn.softmax(jnp.stack(lses, axis=0), axis=0)
    ob = jnp.sum(wts[..., None] * jnp.stack(outs, axis=0).astype(jnp.float32), axis=0)
    ob = ob.astype(h.dtype).reshape(N, S, D_B)

    u = ca * jax.nn.sigmoid(cg)
    u = dwconv(u, conv_c_w, conv_c_b)
    oc = jax.nn.silu(layernorm(u, ln_c_g, ln_c_b))

    return jnp.concatenate([oa, ob, oc], axis=-1) @ w_out


def channel_mixer(h, w_up, conv_f_w, conv_f_b, w_down):
    g, u = jnp.split(h @ w_up, 2, axis=-1)
    g = dwconv(g, conv_f_w, conv_f_b)
    return (jax.nn.silu(g) * u) @ w_down


def trunk(x, c, norm1_g, norm2_g, w_mod, b_mod, w_in, qn_a, kn_a, sink_a, qn_b, kn_b,
          conv_c_w, conv_c_b, ln_c_g, ln_c_b, w_out, w_up, conv_f_w, conv_f_b, w_down):
    for l in range(DEPTH):
        mod = jax.nn.silu(c) @ w_mod[l] + b_mod[l]
        sh1, sc1, g1, sh2, sc2, g2 = jnp.split(mod[:, None, :], 6, axis=-1)
        h = rmsnorm(x, norm1_g[l]) * (1 + sc1) + sh1
        x = x + g1 * token_mixers(h, w_in[l], qn_a[l], kn_a[l], sink_a[l], qn_b[l], kn_b[l],
                                  conv_c_w[l], conv_c_b[l], ln_c_g[l], ln_c_b[l], w_out[l])
        h = rmsnorm(x, norm2_g[l]) * (1 + sc2) + sh2
        x = x + g2 * channel_mixer(h, w_up[l], conv_f_w[l], conv_f_b[l], w_down[l])
    return x


def setup_inputs(seed: int = 0) -> dict:
    key = jax.random.key(seed)
    ks = jax.random.split(key, 26)
    f32 = jnp.float32

    def nrm(k, shape, scale):
        return jax.random.normal(k, shape, f32) * scale

    return {
        "x_prompt": nrm(ks[0], (BATCH, SEQ, D_MODEL), 1.0),
        "x_sample": nrm(ks[1], (DEC_BATCH, DEC_SEQ, D_MODEL), 1.0),
        "c_prompt": nrm(ks[2], (BATCH, D_MODEL), 1.0),
        "c_sample": nrm(ks[3], (DEC_BATCH, D_MODEL), 1.0),
        "norm1_g": 1.0 + nrm(ks[4], (DEPTH, D_MODEL), 0.02),
        "norm2_g": 1.0 + nrm(ks[5], (DEPTH, D_MODEL), 0.02),
        "w_mod": nrm(ks[6], (DEPTH, D_MODEL, 6 * D_MODEL), 0.5 * D_MODEL ** -0.5),
        "b_mod": nrm(ks[7], (DEPTH, 6 * D_MODEL), 0.02),
        "w_in": nrm(ks[8], (DEPTH, D_MODEL, IN_WIDTH), D_MODEL ** -0.5),
        "qn_a": 1.0 + nrm(ks[9], (DEPTH, HEAD_DIM), 0.02),
        "kn_a": 1.0 + nrm(ks[10], (DEPTH, HEAD_DIM), 0.02),
        "sink_a": nrm(ks[11], (DEPTH, A_Q_HEADS), 0.5),
        "qn_b": 1.0 + nrm(ks[12], (DEPTH, HEAD_DIM), 0.02),
        "kn_b": 1.0 + nrm(ks[13], (DEPTH, HEAD_DIM), 0.02),
        "conv_c_w": nrm(ks[14], (DEPTH, C_CONV_WIDTH, C_CH), C_CONV_WIDTH ** -0.5),
        "conv_c_b": nrm(ks[15], (DEPTH, C_CH), 0.02),
        "ln_c_g": 1.0 + nrm(ks[16], (DEPTH, C_CH), 0.02),
        "ln_c_b": nrm(ks[17], (DEPTH, C_CH), 0.02),
        "w_out": nrm(ks[18], (DEPTH, D_MIX, D_MODEL), D_MIX ** -0.5),
        "w_up": nrm(ks[19], (DEPTH, D_MODEL, 2 * FF), D_MODEL ** -0.5),
        "conv_f_w": nrm(ks[20], (DEPTH, FF_CONV_WIDTH, FF), FF_CONV_WIDTH ** -0.5),
        "conv_f_b": nrm(ks[21], (DEPTH, FF), 0.02),
        "w_down": nrm(ks[22], (DEPTH, FF, D_MODEL), FF ** -0.5),
    }


def reference(x_prompt, x_sample, c_prompt, c_sample, norm1_g, norm2_g, w_mod, b_mod, w_in,
              qn_a, kn_a, sink_a, qn_b, kn_b, conv_c_w, conv_c_b, ln_c_g, ln_c_b, w_out,
              w_up, conv_f_w, conv_f_b, w_down):
    y_prompt = trunk(x_prompt, c_prompt, norm1_g, norm2_g, w_mod, b_mod, w_in, qn_a, kn_a, sink_a,
                     qn_b, kn_b, conv_c_w, conv_c_b, ln_c_g, ln_c_b, w_out, w_up, conv_f_w,
                     conv_f_b, w_down)
    y_sample = trunk(x_sample, c_sample, norm1_g, norm2_g, w_mod, b_mod, w_in, qn_a, kn_a, sink_a,
                     qn_b, kn_b, conv_c_w, conv_c_b, ln_c_g, ln_c_b, w_out, w_up, conv_f_w,
                     conv_f_b, w_down)
    return (y_prompt, y_sample)
```

```python
import functools

import numpy as np
import jax
import jax.numpy as jnp
from jax import lax
from jax.experimental import pallas as pl
from jax.experimental.pallas import tpu as pltpu

D_MODEL = 1024
DEPTH = 2
HEAD_DIM = 64
HALF_DIM = HEAD_DIM // 2
LANES = 128
A_Q_HEADS = 4
A_KV_HEADS = 2
A_HALF_WINDOW = 128
B_HEADS = 6
B_PATTERNS = ((128, 1), (512, 4), (2048, 16))
B_HALF = 64
D_A = A_Q_HEADS * HEAD_DIM
D_A_KV = A_KV_HEADS * HEAD_DIM
D_B = B_HEADS * HEAD_DIM
C_CH = 384
C_CONV_WIDTH = 31
C_HALO = 16
D_MIX = D_A + D_B + C_CH
IN_WIDTH = D_A + 2 * D_A_KV + 3 * D_B + 2 * C_CH
FF = 2816
FF_CHUNK = 256
ROPE_THETA = 10000.0
EPS = 1e-6
NEG = -1e30

N_NORM_COLS = D_A + D_A_KV + 2 * D_B
N_V_COLS = D_A_KV + D_B
Q_TILE = 128
ROW_TILE = 512
SUBLANES = 8
VMEM_LIMIT = 56 * 1024 * 1024

BF16 = jnp.bfloat16
F32 = jnp.float32


def _cparams(n_axes, vmem=VMEM_LIMIT):
    return pltpu.CompilerParams(dimension_semantics=("parallel",) * n_axes,
                                vmem_limit_bytes=vmem)


def _lane_iota(shape):
    return lax.broadcasted_iota(jnp.int32, shape, len(shape) - 1)


def _mod_kernel(c_ref, w_ref, b_ref, o_ref):
    c = c_ref[...]
    a = c / (1.0 + jnp.exp(-c))
    w = w_ref[0]
    a_hi = a.astype(BF16)
    a_lo = (a - a_hi.astype(F32)).astype(BF16)
    w_hi = w.astype(BF16)
    w_lo = (w - w_hi.astype(F32)).astype(BF16)
    acc = jnp.dot(a_hi, w_hi, preferred_element_type=F32)
    acc += jnp.dot(a_lo, w_hi, preferred_element_type=F32)
    acc += jnp.dot(a_hi, w_lo, preferred_element_type=F32)
    o_ref[0] = acc + b_ref[0]


def _modulation(c, w_mod, b_mod):
    n = c.shape[0]
    tn = 1536
    return pl.pallas_call(
        _mod_kernel,
        grid=(DEPTH, 6 * D_MODEL // tn),
        in_specs=[pl.BlockSpec((n, D_MODEL), lambda l, j: (0, 0)),
                  pl.BlockSpec((1, D_MODEL, tn), lambda l, j: (l, 0, j)),
                  pl.BlockSpec((1, 1, tn), lambda l, j: (l, 0, j))],
        out_specs=pl.BlockSpec((1, n, tn), lambda l, j: (l, 0, j)),
        out_shape=jax.ShapeDtypeStruct((DEPTH, n, 6 * D_MODEL), F32),
        compiler_params=_cparams(2),
        name="modulation",
    )(c, w_mod, b_mod.reshape(DEPTH, 1, 6 * D_MODEL))


def _adaln(x, g, sc, sh):
    ms = jnp.mean(x * x, axis=-1, keepdims=True)
    return (x * lax.rsqrt(ms + EPS) * g) * (1.0 + sc) + sh


def _inproj_kernel(x_ref, g_ref, sc_ref, sh_ref, w_ref, hg_ref, cos_ref, sin_ref, gm_ref,
                   qa_ref, ka_ref, va_ref, qb_ref, kb_ref, vb_ref, u_ref):
    h = _adaln(x_ref[0], g_ref[...], sc_ref[0], sh_ref[0]).astype(BF16)
    proj = jnp.dot(h, w_ref[...], preferred_element_type=F32)
    cosf = cos_ref[...]
    sinf = sin_ref[...]
    gm = gm_ref[...]

    def norm_rope(c0, width):
        p = proj[:, c0:c0 + width]
        sq = (p * p).astype(BF16)
        ms = jnp.dot(sq, gm[:width, :width], preferred_element_type=F32)
        y = p * lax.rsqrt(ms + EPS) * hg_ref[:, c0:c0 + width]
        outs = []
        for j in range(width // LANES):
            yb = y[:, j * LANES:(j + 1) * LANES]
            outs.append((yb * cosf + pltpu.roll(yb, HEAD_DIM, 1) * sinf).astype(BF16))
        return outs

    blocks = []
    for c0 in range(0, N_NORM_COLS, 2 * LANES):
        blocks += norm_rope(c0, min(2 * LANES, N_NORM_COLS - c0))
    qa_ref[0, 0] = blocks[0]
    qa_ref[0, 1] = blocks[1]
    ka_ref[0] = blocks[2]
    for j in range(3):
        qb_ref[0, j] = blocks[3 + j]
        kb_ref[0, j] = blocks[6 + j]
    v0 = N_NORM_COLS
    va_ref[0] = proj[:, v0:v0 + LANES].astype(BF16)
    for j in range(3):
        vb_ref[0, j] = proj[:, v0 + (1 + j) * LANES:v0 + (2 + j) * LANES].astype(BF16)
    c0 = N_NORM_COLS + N_V_COLS
    ca = proj[:, c0:c0 + C_CH]
    cg = proj[:, c0 + C_CH:c0 + 2 * C_CH]
    u_ref[0] = (ca / (1.0 + jnp.exp(-cg))).astype(BF16)


def _inproj(x, g, sc, sh, w_in, head_gain, cosf, sinf, gmat):
    n, s, _ = x.shape
    tm = ROW_TILE
    row = lambda i, j: (i, j, 0)
    hp_row = lambda i, j: (i, 0, j, 0)
    const2 = lambda i, j: (0, 0)
    per_seq = lambda i, j: (i, 0, 0)
    return pl.pallas_call(
        _inproj_kernel,
        grid=(n, s // tm),
        in_specs=[pl.BlockSpec((1, tm, D_MODEL), row),
                  pl.BlockSpec((1, D_MODEL), const2),
                  pl.BlockSpec((1, 1, D_MODEL), per_seq),
                  pl.BlockSpec((1, 1, D_MODEL), per_seq),
                  pl.BlockSpec((D_MODEL, IN_WIDTH), const2),
                  pl.BlockSpec((1, N_NORM_COLS), const2),
                  pl.BlockSpec((tm, LANES), lambda i, j: (j, 0)),
                  pl.BlockSpec((tm, LANES), lambda i, j: (j, 0)),
                  pl.BlockSpec((2 * LANES, 2 * LANES), const2)],
        out_specs=[pl.BlockSpec((1, 2, tm, LANES), hp_row),
                   pl.BlockSpec((1, tm, LANES), row),
                   pl.BlockSpec((1, tm, LANES), row),
                   pl.BlockSpec((1, 3, tm, LANES), hp_row),
                   pl.BlockSpec((1, 3, tm, LANES), hp_row),
                   pl.BlockSpec((1, 3, tm, LANES), hp_row),
                   pl.BlockSpec((1, tm, C_CH), row)],
        out_shape=[jax.ShapeDtypeStruct((n, 2, s, LANES), BF16),
                   jax.ShapeDtypeStruct((n, s, LANES), BF16),
                   jax.ShapeDtypeStruct((n, s, LANES), BF16),
                   jax.ShapeDtypeStruct((n, 3, s, LANES), BF16),
                   jax.ShapeDtypeStruct((n, 3, s, LANES), BF16),
                   jax.ShapeDtypeStruct((n, 3, s, LANES), BF16),
                   jax.ShapeDtypeStruct((n, s, C_CH), BF16)],
        compiler_params=_cparams(2),
        name="inproj",
    )(x, g, sc, sh, w_in, head_gain, cosf, sinf, gmat)


def _head_masks():
    lane = _lane_iota((1, LANES))
    qk_a = jnp.where((lane % HEAD_DIM) < HALF_DIM, 1.0, 0.0).astype(BF16)
    v_a = lane < HEAD_DIM
    return qk_a, jnp.where(v_a, 1.0, 0.0).astype(BF16), v_a


def _split_heads(q, qk_a):
    return q * qk_a, q * (1.0 - qk_a).astype(BF16)


def _aug_values(v, v_m):
    v_n = (1.0 - v_m).astype(BF16)
    aug_a = jnp.concatenate([v * v_m, jnp.broadcast_to(v_m, v.shape)], axis=1)
    aug_b = jnp.concatenate([v * v_n, jnp.broadcast_to(v_n, v.shape)], axis=1)
    return aug_a, aug_b


def _scores(lhs, k):
    return lax.dot_general(lhs, k, (((1,), (1,)), ((), ())), preferred_element_type=F32)


def _attn_b_kernel(q_ref, k_ref, v_ref, bias_ref, o_ref, lse_ref, *, dil, ls):
    qk_a, v_m, v_a = _head_masks()
    n_tiles = ls // Q_TILE
    kt = 2 * Q_TILE

    def block(r, q0, k0, variant):
        cols = slice(r * LANES, (r + 1) * LANES)
        q = q_ref[0, 0, pl.ds(q0, Q_TILE), cols]
        k = k_ref[0, 0, pl.ds(k0, kt), cols]
        v = v_ref[0, 0, pl.ds(k0, kt), cols]
        qa, qb = _split_heads(q, qk_a)
        bias = bias_ref[variant]
        sa = _scores(qa, k) + bias
        sb = _scores(qb, k) + bias
        ma = jnp.max(sa, axis=1, keepdims=True)
        mb = jnp.max(sb, axis=1, keepdims=True)
        pa = jnp.exp(sa - ma).astype(BF16)
        pb = jnp.exp(sb - mb).astype(BF16)
        aug_a, aug_b = _aug_values(v, v_m)
        res = (jnp.dot(pa, aug_a, preferred_element_type=F32)
               + jnp.dot(pb, aug_b, preferred_element_type=F32))
        den = res[:, LANES:]
        o_ref[0, 0, pl.ds(q0, Q_TILE), cols] = (res[:, :LANES] / den).astype(BF16)
        lse_ref[0, 0, pl.ds(q0, Q_TILE), cols] = jnp.where(v_a, ma, mb) + jnp.log(den)

    for r in range(dil):
        block(r, 0, 0, 0)
        if n_tiles > 2:
            def body(t, carry, r=r):
                q0 = pl.multiple_of(t * Q_TILE, Q_TILE)
                block(r, q0, pl.multiple_of(q0 - B_HALF, B_HALF), 1)
                return carry
            lax.fori_loop(1, n_tiles - 1, body, 0)
        block(r, ls - Q_TILE, ls - kt, 2)


def _attn_b(q, k, v, bias, dil):
    n, _, s, _ = q.shape
    ls = s // dil
    view = lambda t: t.reshape(n, 3, ls, dil * LANES)
    spec = pl.BlockSpec((1, 1, ls, dil * LANES), lambda i, j: (i, j, 0, 0))
    o, lse = pl.pallas_call(
        functools.partial(_attn_b_kernel, dil=dil, ls=ls),
        grid=(n, 3),
        in_specs=[spec, spec, spec,
                  pl.BlockSpec((3, Q_TILE, 2 * Q_TILE), lambda i, j: (0, 0, 0))],
        out_specs=[spec, spec],
        out_shape=[jax.ShapeDtypeStruct((n, 3, ls, dil * LANES), BF16),
                   jax.ShapeDtypeStruct((n, 3, ls, dil * LANES), F32)],
        compiler_params=_cparams(2),
        name=f"attn_b_d{dil}",
    )(view(q), view(k), view(v), bias)
    return o.reshape(n, 3, s, LANES), lse.reshape(n, 3, s, LANES)


def _attn_a_kernel(q_ref, k_ref, v_ref, bias_ref, sink_ref, o_ref, *, s):
    qk_a, v_m, v_a = _head_masks()
    n_tiles = s // Q_TILE
    kt = Q_TILE + 2 * A_HALF_WINDOW

    def block(q0, k0, variant):
        k = k_ref[0, pl.ds(k0, kt), :]
        v = v_ref[0, pl.ds(k0, kt), :]
        aug_a, aug_b = _aug_values(v, v_m)
        bias = bias_ref[variant]
        for c in range(2):
            q = q_ref[0, c, pl.ds(q0, Q_TILE), :]
            qa, qb = _split_heads(q, qk_a)
            sink = sink_ref[c:c + 1, :]
            sa = _scores(qa, k) + bias
            sb = _scores(qb, k) + bias
            ma = jnp.maximum(jnp.max(sa, axis=1, keepdims=True), sink[:, 0:1])
            mb = jnp.maximum(jnp.max(sb, axis=1, keepdims=True), sink[:, LANES - 1:LANES])
            pa = jnp.exp(sa - ma).astype(BF16)
            pb = jnp.exp(sb - mb).astype(BF16)
            res = (jnp.dot(pa, aug_a, preferred_element_type=F32)
                   + jnp.dot(pb, aug_b, preferred_element_type=F32))
            den = res[:, LANES:] + jnp.exp(sink - jnp.where(v_a, ma, mb))
            o_ref[0, c, pl.ds(q0, Q_TILE), :] = (res[:, :LANES] / den).astype(BF16)

    block(0, 0, 0)

    def body(t, carry):
        q0 = pl.multiple_of(t * Q_TILE, Q_TILE)
        block(q0, pl.multiple_of(q0 - A_HALF_WINDOW, A_HALF_WINDOW), 1)
        return carry
    lax.fori_loop(1, n_tiles - 1, body, 0)
    block(s - Q_TILE, s - kt, 2)


def _attn_a(q, k, v, bias, sink_rows):
    n, _, s, _ = q.shape
    kt = Q_TILE + 2 * A_HALF_WINDOW
    return pl.pallas_call(
        functools.partial(_attn_a_kernel, s=s),
        grid=(n,),
        in_specs=[pl.BlockSpec((1, 2, s, LANES), lambda i: (i, 0, 0, 0)),
                  pl.BlockSpec((1, s, LANES), lambda i: (i, 0, 0)),
                  pl.BlockSpec((1, s, LANES), lambda i: (i, 0, 0)),
                  pl.BlockSpec((3, Q_TILE, kt), lambda i: (0, 0, 0)),
                  pl.BlockSpec((2, LANES), lambda i: (0, 0))],
        out_specs=pl.BlockSpec((1, 2, s, LANES), lambda i: (i, 0, 0, 0)),
        out_shape=jax.ShapeDtypeStruct((n, 2, s, LANES), BF16),
        compiler_params=_cparams(1),
        name="attn_a",
    )(q, k, v, bias, sink_rows)


def _band_bias(q_tile, k_tile, half):
    i = np.arange(q_tile)[:, None]
    j = np.arange(k_tile)[None, :]
    out = []
    for delta in (0, -half, -2 * half):
        out.append(np.where(np.abs(j + delta - i) <= half, 0.0, NEG))
    return jnp.asarray(np.stack(out), dtype=F32)


CONV_ROWS = 64
CONV_TILE = 256


def _conv_c_kernel(u_ref, prev_ref, next_ref, w_ref, b_ref, g_ref, beta_ref, o_ref, buf_ref):
    tm = u_ref.shape[1]
    j = pl.program_id(1)
    keep_prev = jnp.where(j > 0, 1.0, 0.0)
    keep_next = jnp.where(j < pl.num_programs(1) - 1, 1.0, 0.0)
    buf_ref[0:C_HALO, :] = prev_ref[0].astype(F32) * keep_prev
    buf_ref[C_HALO:C_HALO + tm, :] = u_ref[0].astype(F32)
    buf_ref[C_HALO + tm:2 * C_HALO + tm, :] = next_ref[0].astype(F32) * keep_next
    pad = (C_CONV_WIDTH - 1) // 2
    for r0 in range(0, tm, CONV_ROWS):
        acc = jnp.broadcast_to(b_ref[...], (CONV_ROWS, C_CH))
        for kk in range(C_CONV_WIDTH):
            lo = r0 + C_HALO - pad + kk
            acc = acc + buf_ref[lo:lo + CONV_ROWS, :] * w_ref[kk:kk + 1, :]
        mu = jnp.mean(acc, axis=-1, keepdims=True)
        cen = acc - mu
        var = jnp.mean(cen * cen, axis=-1, keepdims=True)
        y = cen * lax.rsqrt(var + EPS) * g_ref[...] + beta_ref[...]
        o_ref[0, r0:r0 + CONV_ROWS, :] = (y / (1.0 + jnp.exp(-y))).astype(BF16)


def _conv_c(u, w, b, g, beta):
    n, s, _ = u.shape
    tm = CONV_TILE
    hb = tm // C_HALO
    last = s // C_HALO - 1
    vec = pl.BlockSpec((1, C_CH), lambda i, j: (0, 0))
    return pl.pallas_call(
        _conv_c_kernel,
        grid=(n, s // tm),
        in_specs=[pl.BlockSpec((1, tm, C_CH), lambda i, j: (i, j, 0)),
                  pl.BlockSpec((1, C_HALO, C_CH), lambda i, j: (i, jnp.maximum(j * hb - 1, 0), 0)),
                  pl.BlockSpec((1, C_HALO, C_CH), lambda i, j: (i, jnp.minimum((j + 1) * hb, last), 0)),
                  pl.BlockSpec((C_CONV_WIDTH, C_CH), lambda i, j: (0, 0)),
                  vec, vec, vec],
        out_specs=pl.BlockSpec((1, tm, C_CH), lambda i, j: (i, j, 0)),
        out_shape=jax.ShapeDtypeStruct((n, s, C_CH), BF16),
        scratch_shapes=[pltpu.VMEM((tm + 2 * C_HALO, C_CH), F32)],
        compiler_params=_cparams(2),
        name="conv_c",
    )(u, u, u, w, b.reshape(1, C_CH), g.reshape(1, C_CH), beta.reshape(1, C_CH))


def _outproj_kernel(x_ref, gate_ref, oa_ref, o1_ref, o2_ref, o3_ref, l1_ref, l2_ref, l3_ref,
                    oc_ref, w_ref, y_ref):
    cols = [oa_ref[0, 0], oa_ref[0, 1]]
    for j in range(3):
        l1, l2, l3 = l1_ref[0, j], l2_ref[0, j], l3_ref[0, j]
        m = jnp.maximum(jnp.maximum(l1, l2), l3)
        e1, e2, e3 = jnp.exp(l1 - m), jnp.exp(l2 - m), jnp.exp(l3 - m)
        den = e1 + e2 + e3
        mixed = (e1 / den) * o1_ref[0, j].astype(F32)
        mixed += (e2 / den) * o2_ref[0, j].astype(F32)
        mixed += (e3 / den) * o3_ref[0, j].astype(F32)
        cols.append(mixed.astype(BF16))
    cols.append(oc_ref[0])
    mix = jnp.concatenate(cols, axis=1)
    y_ref[0] = x_ref[0] + gate_ref[0] * jnp.dot(mix, w_ref[...], preferred_element_type=F32)


def _outproj(x, gate, oa, ob, lse, oc, w_out):
    n, s, _ = x.shape
    tm = ROW_TILE
    row = lambda i, j: (i, j, 0)
    hp_row = lambda i, j: (i, 0, j, 0)
    hp3 = pl.BlockSpec((1, 3, tm, LANES), hp_row)
    return pl.pallas_call(
        _outproj_kernel,
        grid=(n, s // tm),
        in_specs=[pl.BlockSpec((1, tm, D_MODEL), row),
                  pl.BlockSpec((1, 1, D_MODEL), lambda i, j: (i, 0, 0)),
                  pl.BlockSpec((1, 2, tm, LANES), hp_row),
                  hp3, hp3, hp3, hp3, hp3, hp3,
                  pl.BlockSpec((1, tm, C_CH), row),
                  pl.BlockSpec((D_MIX, D_MODEL), lambda i, j: (0, 0))],
        out_specs=pl.BlockSpec((1, tm, D_MODEL), row),
        out_shape=jax.ShapeDtypeStruct((n, s, D_MODEL), F32),
        compiler_params=_cparams(2),
        name="outproj",
    )(x, gate, oa, ob[0], ob[1], ob[2], lse[0], lse[1], lse[2], oc, w_out)


def _ffn_kernel(x_ref, prev_ref, next_ref, g_ref, sc_ref, sh_ref, gate_ref, wup_ref, cw_ref, cb_ref,
                wdn_ref, y_ref, gbuf_ref, act_ref):
    tm = x_ref.shape[1]
    j = pl.program_id(1)
    g, sc, sh = g_ref[...], sc_ref[0], sh_ref[0]
    x = x_ref[0]
    h_mid = _adaln(x, g, sc, sh)
    keep_prev = jnp.where(j > 0, 1.0, 0.0)
    keep_next = jnp.where(j < pl.num_programs(1) - 1, 1.0, 0.0)
    h_prev = _adaln(prev_ref[0], g, sc, sh) * keep_prev
    h_next = _adaln(next_ref[0], g, sc, sh) * keep_next
    h_ext = jnp.concatenate([h_prev, h_mid, h_next], axis=0).astype(BF16)
    h_bf = h_mid.astype(BF16)
    for c in range(FF // FF_CHUNK):
        lo = c * FF_CHUNK
        gbuf_ref[...] = jnp.dot(h_ext, wup_ref[:, lo:lo + FF_CHUNK], preferred_element_type=F32)
        up = jnp.dot(h_bf, wup_ref[:, FF + lo:FF + lo + FF_CHUNK], preferred_element_type=F32)
        cw = cw_ref[:, lo:lo + FF_CHUNK]
        conv = (gbuf_ref[SUBLANES - 1:SUBLANES - 1 + tm, :] * cw[0:1, :]
                + gbuf_ref[SUBLANES:SUBLANES + tm, :] * cw[1:2, :]
                + gbuf_ref[SUBLANES + 1:SUBLANES + 1 + tm, :] * cw[2:3, :]
                + cb_ref[:, lo:lo + FF_CHUNK])
        act_ref[:, lo:lo + FF_CHUNK] = ((conv / (1.0 + jnp.exp(-conv))) * up).astype(BF16)
    down = jnp.dot(act_ref[...], wdn_ref[...], preferred_element_type=F32)
    y_ref[0] = x + gate_ref[0] * down


def _ffn(x, g, sc, sh, gate, w_up, conv_w, conv_b, w_down):
    n, s, _ = x.shape
    tm = ROW_TILE
    hb = tm // SUBLANES
    last = s // SUBLANES - 1
    per_seq = lambda i, j: (i, 0, 0)
    const2 = lambda i, j: (0, 0)
    once = pl.Buffered(1)
    return pl.pallas_call(
        _ffn_kernel,
        grid=(n, s // tm),
        in_specs=[pl.BlockSpec((1, tm, D_MODEL), lambda i, j: (i, j, 0)),
                  pl.BlockSpec((1, SUBLANES, D_MODEL), lambda i, j: (i, jnp.maximum(j * hb - 1, 0), 0)),
                  pl.BlockSpec((1, SUBLANES, D_MODEL), lambda i, j: (i, jnp.minimum((j + 1) * hb, last), 0)),
                  pl.BlockSpec((1, D_MODEL), const2),
                  pl.BlockSpec((1, 1, D_MODEL), per_seq),
                  pl.BlockSpec((1, 1, D_MODEL), per_seq),
                  pl.BlockSpec((1, 1, D_MODEL), per_seq),
                  pl.BlockSpec((D_MODEL, 2 * FF), const2, pipeline_mode=once),
                  pl.BlockSpec((3, FF), const2),
                  pl.BlockSpec((1, FF), const2),
                  pl.BlockSpec((FF, D_MODEL), const2, pipeline_mode=once)],
        out_specs=pl.BlockSpec((1, tm, D_MODEL), lambda i, j: (i, j, 0)),
        out_shape=jax.ShapeDtypeStruct((n, s, D_MODEL), F32),
        scratch_shapes=[pltpu.VMEM((tm + 2 * SUBLANES, FF_CHUNK), F32),
                        pltpu.VMEM((tm, FF), BF16)],
        compiler_params=_cparams(2),
        name="ffn",
    )(x, x, x, g, sc, sh, gate, w_up, conv_w, conv_b.reshape(1, FF), w_down)


def _pair_cols(base, head_a, head_b):
    a = base + head_a * HEAD_DIM
    b = base + head_b * HEAD_DIM
    lo = np.arange(HALF_DIM)
    return np.concatenate([a + lo, b + lo, a + HALF_DIM + lo, b + HALF_DIM + lo])


def _w_in_perm():
    qa, ka, va = 0, D_A, D_A + D_A_KV
    qb = D_A + 2 * D_A_KV
    kb, vb = qb + D_B, qb + 2 * D_B
    ca = qb + 3 * D_B
    cols = [_pair_cols(qa, 0, 2), _pair_cols(qa, 1, 3), _pair_cols(ka, 0, 1)]
    cols += [_pair_cols(qb, 2 * j, 2 * j + 1) for j in range(3)]
    cols += [_pair_cols(kb, 2 * j, 2 * j + 1) for j in range(3)]
    cols += [np.arange(va, va + D_A_KV), np.arange(vb, vb + D_B), np.arange(ca, ca + 2 * C_CH)]
    return np.concatenate(cols)


def _w_out_perm():
    heads = [0, 2, 1, 3]
    rows = [np.arange(h * HEAD_DIM, (h + 1) * HEAD_DIM) for h in heads]
    return np.concatenate(rows + [np.arange(D_A, D_MIX)])


def _pair_gain(g, scale):
    return jnp.concatenate([g[:HALF_DIM], g[:HALF_DIM], g[HALF_DIM:], g[HALF_DIM:]]) * scale


def _head_mean_matrix():
    lane = np.arange(2 * LANES)
    head = (lane // LANES) * 2 + ((lane % HEAD_DIM) >= HALF_DIM)
    return jnp.asarray((head[:, None] == head[None, :]) / HEAD_DIM, dtype=BF16)


def _rope_tables(s):
    inv = 1.0 / (jnp.float32(ROPE_THETA) ** (jnp.arange(HALF_DIM, dtype=F32) / HALF_DIM))
    ang = jnp.arange(s, dtype=F32)[:, None] * inv[None, :]
    cos, sin = jnp.cos(ang), jnp.sin(ang)
    return (jnp.concatenate([cos, cos, cos, cos], axis=1),
            jnp.concatenate([-sin, -sin, sin, sin], axis=1))


def kernel(x_prompt, x_sample, c_prompt, c_sample, norm1_g, norm2_g, w_mod, b_mod, w_in, qn_a, kn_a,
           sink_a, qn_b, kn_b, conv_c_w, conv_c_b, ln_c_g, ln_c_b, w_out, w_up, conv_f_w, conv_f_b,
           w_down):
    n_prompt = x_prompt.shape[0]
    x = jnp.concatenate([x_prompt, x_sample], axis=0)
    c = jnp.concatenate([c_prompt, c_sample], axis=0)
    n, s, _ = x.shape

    mod = _modulation(c, w_mod, b_mod).reshape(DEPTH, n, 6, 1, D_MODEL)
    cosf, sinf = _rope_tables(s)
    gmat = _head_mean_matrix()
    bias_a = _band_bias(Q_TILE, Q_TILE + 2 * A_HALF_WINDOW, A_HALF_WINDOW)
    bias_b = _band_bias(Q_TILE, Q_TILE + 2 * B_HALF, B_HALF)
    in_perm, out_perm = _w_in_perm(), _w_out_perm()
    q_scale = 1.0 / np.sqrt(HEAD_DIM)

    for l in range(DEPTH):
        sh1, sc1, g1, sh2, sc2, g2 = (mod[l, :, i] for i in range(6))
        head_gain = jnp.concatenate(
            [_pair_gain(qn_a[l], q_scale)] * 2 + [_pair_gain(kn_a[l], 1.0)]
            + [_pair_gain(qn_b[l], q_scale)] * 3 + [_pair_gain(kn_b[l], 1.0)] * 3).reshape(1, N_NORM_COLS)
        sink = sink_a[l]
        sink_rows = jnp.stack([
            jnp.concatenate([jnp.full((HEAD_DIM,), sink[0]), jnp.full((HEAD_DIM,), sink[2])]),
            jnp.concatenate([jnp.full((HEAD_DIM,), sink[1]), jnp.full((HEAD_DIM,), sink[3])])])

        qa, ka, va, qb, kb, vb, u = _inproj(
            x, norm1_g[l].reshape(1, D_MODEL), sc1, sh1, w_in[l][:, in_perm].astype(BF16),
            head_gain, cosf, sinf, gmat)
        oa = _attn_a(qa, ka, va, bias_a, sink_rows)
        ob, lse = zip(*[_attn_b(qb, kb, vb, bias_b, dil) for _, dil in B_PATTERNS])
        oc = _conv_c(u, conv_c_w[l], conv_c_b[l], ln_c_g[l], ln_c_b[l])
        x = _outproj(x, g1, oa, ob, lse, oc, w_out[l][out_perm, :].astype(BF16))
        x = _ffn(x, norm2_g[l].reshape(1, D_MODEL), sc2, sh2, g2, w_up[l].astype(BF16),
                 conv_f_w[l], conv_f_b[l], w_down[l].astype(BF16))
    return x[:n_prompt], x[n_prompt:]
```

```python
import functools

import numpy as np
import jax
import jax.numpy as jnp
from jax import lax
from jax.experimental import pallas as pl
from jax.experimental.pallas import tpu as pltpu

D_MODEL = 1024
DEPTH = 2
HEAD_DIM = 64
HALF_DIM = HEAD_DIM // 2
LANES = 128
SUBLANES = 8
A_Q_HEADS = 4
A_KV_HEADS = 2
A_HALF_WINDOW = 128
B_HEADS = 6
B_PATTERNS = ((128, 1), (512, 4), (2048, 16))
B_HALF = 64
D_A = A_Q_HEADS * HEAD_DIM
D_A_KV = A_KV_HEADS * HEAD_DIM
D_B = B_HEADS * HEAD_DIM
C_CH = 384
C_CONV_WIDTH = 31
C_HALO = 16
D_MIX = D_A + D_B + C_CH
IN_WIDTH = D_A + 2 * D_A_KV + 3 * D_B + 2 * C_CH
FF = 2816
FF_CHUNK = 2 * LANES
ROPE_THETA = 10000.0
EPS = 1e-6
NEG = -1e30

N_NORM_COLS = D_A + D_A_KV + 2 * D_B
N_V_COLS = D_A_KV + D_B
Q_TILE = 128
ROW_TILE = 512
COPY_ROWS = 256
BLOCK_UNROLL_A = 4
BLOCK_UNROLL_B = 8
VMEM_LIMIT = 56 * 1024 * 1024

BF16 = jnp.bfloat16
F32 = jnp.float32


def _cparams(n_axes, vmem=VMEM_LIMIT):
    return pltpu.CompilerParams(dimension_semantics=("parallel",) * n_axes,
                                vmem_limit_bytes=vmem)


def _lane_iota(shape):
    return lax.broadcasted_iota(jnp.int32, shape, len(shape) - 1)


def _mod_kernel(c_ref, w_ref, b_ref, o_ref):
    c = c_ref[...]
    a = c / (1.0 + jnp.exp(-c))
    w = w_ref[0]
    a_hi = a.astype(BF16)
    a_lo = (a - a_hi.astype(F32)).astype(BF16)
    w_hi = w.astype(BF16)
    w_lo = (w - w_hi.astype(F32)).astype(BF16)
    acc = jnp.dot(a_hi, w_hi, preferred_element_type=F32)
    acc += jnp.dot(a_lo, w_hi, preferred_element_type=F32)
    acc += jnp.dot(a_hi, w_lo, preferred_element_type=F32)
    o_ref[0] = acc + b_ref[0]


def _modulation(c, w_mod, b_mod):
    n = c.shape[0]
    tn = 1536
    return pl.pallas_call(
        _mod_kernel,
        grid=(DEPTH, 6 * D_MODEL // tn),
        in_specs=[pl.BlockSpec((n, D_MODEL), lambda l, j: (0, 0)),
                  pl.BlockSpec((1, D_MODEL, tn), lambda l, j: (l, 0, j)),
                  pl.BlockSpec((1, 1, tn), lambda l, j: (l, 0, j))],
        out_specs=pl.BlockSpec((1, n, tn), lambda l, j: (l, 0, j)),
        out_shape=jax.ShapeDtypeStruct((DEPTH, n, 6 * D_MODEL), F32),
        compiler_params=_cparams(2),
        name="modulation",
    )(c, w_mod, b_mod.reshape(DEPTH, 1, 6 * D_MODEL))


def _adaln(x, g, sc, sh):
    ms = jnp.mean(x * x, axis=-1, keepdims=True)
    return (x * lax.rsqrt(ms + EPS) * g) * (1.0 + sc) + sh


def _inproj_kernel(x_ref, g_ref, sc_ref, sh_ref, w_ref, hg_ref, cos_ref, sin_ref, gm_ref,
                   qa_ref, ka_ref, va_ref, qb_ref, kb_ref, vb_ref, u_ref):
    h = _adaln(x_ref[0], g_ref[...], sc_ref[0], sh_ref[0]).astype(BF16)
    proj = jnp.dot(h, w_ref[...], preferred_element_type=F32)
    cosf = cos_ref[...]
    sinf = sin_ref[...]
    gm = gm_ref[...]

    def norm_rope(c0, width):
        p = proj[:, c0:c0 + width]
        sq = (p * p).astype(BF16)
        ms = jnp.dot(sq, gm[:width, :width], preferred_element_type=F32)
        y = p * lax.rsqrt(ms + EPS) * hg_ref[:, c0:c0 + width]
        outs = []
        for j in range(width // LANES):
            yb = y[:, j * LANES:(j + 1) * LANES]
            outs.append(yb * cosf + pltpu.roll(yb, HEAD_DIM, 1) * sinf)
        return outs

    blocks = []
    for c0 in range(0, N_NORM_COLS, 2 * LANES):
        blocks += norm_rope(c0, min(2 * LANES, N_NORM_COLS - c0))
    qa_ref[0, 0] = blocks[0].astype(BF16)
    qa_ref[0, 1] = blocks[1].astype(BF16)
    ka_ref[0] = blocks[2].astype(BF16)
    for j in range(3):
        qb_ref[0, j] = blocks[3 + j]
        kb_ref[0, j] = blocks[6 + j]
    v0 = N_NORM_COLS
    va_ref[0] = proj[:, v0:v0 + LANES].astype(BF16)
    for j in range(3):
        vb_ref[0, j] = proj[:, v0 + (1 + j) * LANES:v0 + (2 + j) * LANES]
    c0 = N_NORM_COLS + N_V_COLS
    ca = proj[:, c0:c0 + C_CH]
    cg = proj[:, c0 + C_CH:c0 + 2 * C_CH]
    u_ref[0] = (ca / (1.0 + jnp.exp(-cg))).astype(BF16)


def _inproj(x, g, sc, sh, w_in, head_gain, cosf, sinf, gmat):
    n, s, _ = x.shape
    tm = ROW_TILE
    row = lambda i, j: (i, j, 0)
    hp_row = lambda i, j: (i, 0, j, 0)
    const2 = lambda i, j: (0, 0)
    per_seq = lambda i, j: (i, 0, 0)
    return pl.pallas_call(
        _inproj_kernel,
        grid=(n, s // tm),
        in_specs=[pl.BlockSpec((1, tm, D_MODEL), row),
                  pl.BlockSpec((1, D_MODEL), const2),
                  pl.BlockSpec((1, 1, D_MODEL), per_seq),
                  pl.BlockSpec((1, 1, D_MODEL), per_seq),
                  pl.BlockSpec((D_MODEL, IN_WIDTH), const2),
                  pl.BlockSpec((1, N_NORM_COLS), const2),
                  pl.BlockSpec((tm, LANES), lambda i, j: (j, 0)),
                  pl.BlockSpec((tm, LANES), lambda i, j: (j, 0)),
                  pl.BlockSpec((2 * LANES, 2 * LANES), const2)],
        out_specs=[pl.BlockSpec((1, 2, tm, LANES), hp_row),
                   pl.BlockSpec((1, tm, LANES), row),
                   pl.BlockSpec((1, tm, LANES), row),
                   pl.BlockSpec((1, 3, tm, LANES), hp_row),
                   pl.BlockSpec((1, 3, tm, LANES), hp_row),
                   pl.BlockSpec((1, 3, tm, LANES), hp_row),
                   pl.BlockSpec((1, tm, C_CH), row)],
        out_shape=[jax.ShapeDtypeStruct((n, 2, s, LANES), BF16),
                   jax.ShapeDtypeStruct((n, s, LANES), BF16),
                   jax.ShapeDtypeStruct((n, s, LANES), BF16),
                   jax.ShapeDtypeStruct((n, 3, s, LANES), F32),
                   jax.ShapeDtypeStruct((n, 3, s, LANES), F32),
                   jax.ShapeDtypeStruct((n, 3, s, LANES), F32),
                   jax.ShapeDtypeStruct((n, s, C_CH), BF16)],
        compiler_params=_cparams(2),
        name="inproj",
    )(x, g, sc, sh, w_in, head_gain, cosf, sinf, gmat)


def _head_masks():
    lane = _lane_iota((1, LANES))
    qk_a = jnp.where((lane % HEAD_DIM) < HALF_DIM, 1.0, 0.0).astype(BF16)
    v_a = lane < HEAD_DIM
    return qk_a, jnp.where(v_a, 1.0, 0.0).astype(BF16), v_a


def _split_heads(q, qk_a):
    return q * qk_a, q * (1.0 - qk_a).astype(BF16)


def _aug_values(v, v_m):
    v_n = (1.0 - v_m).astype(BF16)
    aug_a = jnp.concatenate([v * v_m, jnp.broadcast_to(v_m, v.shape)], axis=1)
    aug_b = jnp.concatenate([v * v_n, jnp.broadcast_to(v_n, v.shape)], axis=1)
    return aug_a, aug_b


def _scores(lhs, k):
    return lax.dot_general(lhs, k, (((1,), (1,)), ((), ())), preferred_element_type=F32)


def _band_window(t, n_tiles, half, k_tile, length):
    k0 = jnp.clip(t * Q_TILE - half, 0, length - k_tile)
    variant = jnp.where(t == 0, 0, jnp.where(t == n_tiles - 1, 2, 1))
    return k0, variant


def _attn_b_kernel(q_ref, k_ref, v_ref, bias_ref, o_ref,
                   qd_ref, kd_ref, vd_ref, osub_ref, lsub_ref, onat_ref, lnat_ref, *, s):
    qk_a, v_m, v_a = _head_masks()
    kt = Q_TILE + 2 * B_HALF
    n_blocks = s // Q_TILE

    for pi, (_, dil) in enumerate(B_PATTERNS):
        ls = s // dil
        nt = ls // Q_TILE
        rows = min(COPY_ROWS, ls)
        for r in range(dil):
            for c0 in range(0, ls, rows):
                src = pl.ds(r + c0 * dil, rows, stride=dil) if dil > 1 else pl.ds(c0, rows)
                dst = pl.ds(r * ls + c0, rows)
                qd_ref[dst, :] = q_ref[0, 0, src, :].astype(BF16)
                kd_ref[dst, :] = k_ref[0, 0, src, :].astype(BF16)
                vd_ref[dst, :] = v_ref[0, 0, src, :].astype(BF16)

        o_dst = onat_ref.at[pi] if dil == 1 else osub_ref
        l_dst = lnat_ref.at[pi] if dil == 1 else lsub_ref

        def block(b, carry, ls=ls, nt=nt, o_dst=o_dst, l_dst=l_dst):
            t = b % nt
            kl, variant = _band_window(t, nt, B_HALF, kt, ls)
            q0 = pl.multiple_of(b * Q_TILE, Q_TILE)
            k0 = pl.multiple_of((b // nt) * ls + kl, B_HALF)
            q = qd_ref[pl.ds(q0, Q_TILE), :]
            k = kd_ref[pl.ds(k0, kt), :]
            v = vd_ref[pl.ds(k0, kt), :]
            qa, qb = _split_heads(q, qk_a)
            bias = bias_ref[variant]
            sa = _scores(qa, k) + bias
            sb = _scores(qb, k) + bias
            ma = jnp.max(sa, axis=1, keepdims=True)
            mb = jnp.max(sb, axis=1, keepdims=True)
            pa = jnp.exp(sa - ma).astype(BF16)
            pb = jnp.exp(sb - mb).astype(BF16)
            aug_a, aug_b = _aug_values(v, v_m)
            res = (jnp.dot(pa, aug_a, preferred_element_type=F32)
                   + jnp.dot(pb, aug_b, preferred_element_type=F32))
            den = res[:, LANES:]
            o_dst[pl.ds(q0, Q_TILE), :] = res[:, :LANES] / den
            l_dst[pl.ds(q0, Q_TILE), :] = jnp.where(v_a, ma, mb) + jnp.log(den)
            return carry
        lax.fori_loop(0, n_blocks, block, 0, unroll=BLOCK_UNROLL_B)

        if dil > 1:
            for r in range(dil):
                for c0 in range(0, ls, rows):
                    src = pl.ds(r * ls + c0, rows)
                    dst = pl.ds(r + c0 * dil, rows, stride=dil)
                    onat_ref[pi, dst, :] = osub_ref[src, :]
                    lnat_ref[pi, dst, :] = lsub_ref[src, :]

    def merge(c, carry):
        r0 = pl.multiple_of(c * Q_TILE, Q_TILE)
        sl = pl.ds(r0, Q_TILE)
        l1, l2, l3 = lnat_ref[0, sl, :], lnat_ref[1, sl, :], lnat_ref[2, sl, :]
        m = jnp.maximum(jnp.maximum(l1, l2), l3)
        e1, e2, e3 = jnp.exp(l1 - m), jnp.exp(l2 - m), jnp.exp(l3 - m)
        mixed = e1 * onat_ref[0, sl, :] + e2 * onat_ref[1, sl, :] + e3 * onat_ref[2, sl, :]
        o_ref[0, 0, sl, :] = (mixed / (e1 + e2 + e3)).astype(BF16)
        return carry
    lax.fori_loop(0, n_blocks, merge, 0, unroll=2)


def _attn_b(q, k, v, bias):
    n, _, s, _ = q.shape
    spec = pl.BlockSpec((1, 1, s, LANES), lambda i, j: (i, j, 0, 0))
    return pl.pallas_call(
        functools.partial(_attn_b_kernel, s=s),
        grid=(n, 3),
        in_specs=[spec, spec, spec,
                  pl.BlockSpec((3, Q_TILE, Q_TILE + 2 * B_HALF), lambda i, j: (0, 0, 0))],
        out_specs=spec,
        out_shape=jax.ShapeDtypeStruct((n, 3, s, LANES), BF16),
        scratch_shapes=[pltpu.VMEM((s, LANES), BF16), pltpu.VMEM((s, LANES), BF16),
                        pltpu.VMEM((s, LANES), BF16),
                        pltpu.VMEM((s, LANES), F32), pltpu.VMEM((s, LANES), F32),
                        pltpu.VMEM((3, s, LANES), F32), pltpu.VMEM((3, s, LANES), F32)],
        compiler_params=_cparams(2),
        name="attn_b",
    )(q, k, v, bias)


def _attn_a_kernel(q_ref, k_ref, v_ref, bias_ref, sink_ref, o_ref, *, s):
    qk_a, v_m, v_a = _head_masks()
    n_tiles = s // Q_TILE
    kt = Q_TILE + 2 * A_HALF_WINDOW

    def block(t, carry):
        kl, variant = _band_window(t, n_tiles, A_HALF_WINDOW, kt, s)
        q0 = pl.multiple_of(t * Q_TILE, Q_TILE)
        k0 = pl.multiple_of(kl, A_HALF_WINDOW)
        k = k_ref[0, pl.ds(k0, kt), :]
        v = v_ref[0, pl.ds(k0, kt), :]
        aug_a, aug_b = _aug_values(v, v_m)
        bias = bias_ref[variant]
        for c in range(2):
            q = q_ref[0, c, pl.ds(q0, Q_TILE), :]
            qa, qb = _split_heads(q, qk_a)
            sink = sink_ref[c:c + 1, :]
            sa = _scores(qa, k) + bias
            sb = _scores(qb, k) + bias
            ma = jnp.maximum(jnp.max(sa, axis=1, keepdims=True), sink[:, 0:1])
            mb = jnp.maximum(jnp.max(sb, axis=1, keepdims=True), sink[:, LANES - 1:LANES])
            pa = jnp.exp(sa - ma).astype(BF16)
            pb = jnp.exp(sb - mb).astype(BF16)
            res = (jnp.dot(pa, aug_a, preferred_element_type=F32)
                   + jnp.dot(pb, aug_b, preferred_element_type=F32))
            den = res[:, LANES:] + jnp.exp(sink - jnp.where(v_a, ma, mb))
            o_ref[0, c, pl.ds(q0, Q_TILE), :] = (res[:, :LANES] / den).astype(BF16)
        return carry
    lax.fori_loop(0, n_tiles, block, 0, unroll=BLOCK_UNROLL_A)


def _attn_a(q, k, v, bias, sink_rows):
    n, _, s, _ = q.shape
    kt = Q_TILE + 2 * A_HALF_WINDOW
    return pl.pallas_call(
        functools.partial(_attn_a_kernel, s=s),
        grid=(n,),
        in_specs=[pl.BlockSpec((1, 2, s, LANES), lambda i: (i, 0, 0, 0)),
                  pl.BlockSpec((1, s, LANES), lambda i: (i, 0, 0)),
                  pl.BlockSpec((1, s, LANES), lambda i: (i, 0, 0)),
                  pl.BlockSpec((3, Q_TILE, kt), lambda i: (0, 0, 0)),
                  pl.BlockSpec((2, LANES), lambda i: (0, 0))],
        out_specs=pl.BlockSpec((1, 2, s, LANES), lambda i: (i, 0, 0, 0)),
        out_shape=jax.ShapeDtypeStruct((n, 2, s, LANES), BF16),
        compiler_params=_cparams(1),
        name="attn_a",
    )(q, k, v, bias, sink_rows)


def _band_bias(q_tile, k_tile, half):
    i = np.arange(q_tile)[:, None]
    j = np.arange(k_tile)[None, :]
    out = []
    for delta in (0, -half, -2 * half):
        out.append(np.where(np.abs(j + delta - i) <= half, 0.0, NEG))
    return jnp.asarray(np.stack(out), dtype=F32)


CONV_ROWS = 64
CONV_TILE = 256
C_BLOCKS = C_CH // LANES


def _conv_c_kernel(u_ref, prev_ref, next_ref, w_ref, b_ref, g_ref, beta_ref, o_ref, buf_ref):
    tm = u_ref.shape[1]
    j = pl.program_id(1)
    keep_prev = jnp.where(j > 0, 1.0, 0.0)
    keep_next = jnp.where(j < pl.num_programs(1) - 1, 1.0, 0.0)
    for cb in range(C_BLOCKS):
        cols = slice(cb * LANES, (cb + 1) * LANES)
        buf_ref[cb, 0:C_HALO, :] = prev_ref[0, :, cols].astype(F32) * keep_prev
        buf_ref[cb, C_HALO:C_HALO + tm, :] = u_ref[0, :, cols].astype(F32)
        buf_ref[cb, C_HALO + tm:2 * C_HALO + tm, :] = next_ref[0, :, cols].astype(F32) * keep_next
    pad = (C_CONV_WIDTH - 1) // 2
    for r0 in range(0, tm, CONV_ROWS):
        accs = []
        for cb in range(C_BLOCKS):
            cols = slice(cb * LANES, (cb + 1) * LANES)
            acc = jnp.broadcast_to(b_ref[:, cols], (CONV_ROWS, LANES))
            for kk in range(C_CONV_WIDTH):
                lo = r0 + C_HALO - pad + kk
                acc = acc + buf_ref[cb, lo:lo + CONV_ROWS, :] * w_ref[kk:kk + 1, cols]
            accs.append(acc)
        acc = jnp.concatenate(accs, axis=1)
        mu = jnp.mean(acc, axis=-1, keepdims=True)
        cen = acc - mu
        var = jnp.mean(cen * cen, axis=-1, keepdims=True)
        y = cen * lax.rsqrt(var + EPS) * g_ref[...] + beta_ref[...]
        o_ref[0, r0:r0 + CONV_ROWS, :] = (y / (1.0 + jnp.exp(-y))).astype(BF16)


def _conv_c(u, w, b, g, beta):
    n, s, _ = u.shape
    tm = CONV_TILE
    hb = tm // C_HALO
    last = s // C_HALO - 1
    vec = pl.BlockSpec((1, C_CH), lambda i, j: (0, 0))
    return pl.pallas_call(
        _conv_c_kernel,
        grid=(n, s // tm),
        in_specs=[pl.BlockSpec((1, tm, C_CH), lambda i, j: (i, j, 0)),
                  pl.BlockSpec((1, C_HALO, C_CH), lambda i, j: (i, jnp.maximum(j * hb - 1, 0), 0)),
                  pl.BlockSpec((1, C_HALO, C_CH), lambda i, j: (i, jnp.minimum((j + 1) * hb, last), 0)),
                  pl.BlockSpec((C_CONV_WIDTH, C_CH), lambda i, j: (0, 0)),
                  vec, vec, vec],
        out_specs=pl.BlockSpec((1, tm, C_CH), lambda i, j: (i, j, 0)),
        out_shape=jax.ShapeDtypeStruct((n, s, C_CH), BF16),
        scratch_shapes=[pltpu.VMEM((C_BLOCKS, tm + 2 * C_HALO, LANES), F32)],
        compiler_params=_cparams(2),
        name="conv_c",
    )(u, u, u, w, b.reshape(1, C_CH), g.reshape(1, C_CH), beta.reshape(1, C_CH))


def _outproj_kernel(x_ref, gate_ref, oa_ref, ob_ref, oc_ref, w_ref, y_ref):
    cols = [oa_ref[0, 0], oa_ref[0, 1], ob_ref[0, 0], ob_ref[0, 1], ob_ref[0, 2], oc_ref[0]]
    mix = jnp.concatenate(cols, axis=1)
    y_ref[0] = x_ref[0] + gate_ref[0] * jnp.dot(mix, w_ref[...], preferred_element_type=F32)


def _outproj(x, gate, oa, ob, oc, w_out):
    n, s, _ = x.shape
    tm = ROW_TILE
    row = lambda i, j: (i, j, 0)
    hp_row = lambda i, j: (i, 0, j, 0)
    return pl.pallas_call(
        _outproj_kernel,
        grid=(n, s // tm),
        in_specs=[pl.BlockSpec((1, tm, D_MODEL), row),
                  pl.BlockSpec((1, 1, D_MODEL), lambda i, j: (i, 0, 0)),
                  pl.BlockSpec((1, 2, tm, LANES), hp_row),
                  pl.BlockSpec((1, 3, tm, LANES), hp_row),
                  pl.BlockSpec((1, tm, C_CH), row),
                  pl.BlockSpec((D_MIX, D_MODEL), lambda i, j: (0, 0))],
        out_specs=pl.BlockSpec((1, tm, D_MODEL), row),
        out_shape=jax.ShapeDtypeStruct((n, s, D_MODEL), F32),
        compiler_params=_cparams(2),
        name="outproj",
    )(x, gate, oa, ob, oc, w_out)


def _ffn_kernel(x_ref, prev_ref, next_ref, g_ref, sc_ref, sh_ref, gate_ref, wup_ref, cw_ref, cb_ref,
                wdn_ref, y_ref, gbuf_ref, act_ref):
    tm = x_ref.shape[1]
    j = pl.program_id(1)
    g, sc, sh = g_ref[...], sc_ref[0], sh_ref[0]
    x = x_ref[0]
    h_mid = _adaln(x, g, sc, sh)
    keep_prev = jnp.where(j > 0, 1.0, 0.0)
    keep_next = jnp.where(j < pl.num_programs(1) - 1, 1.0, 0.0)
    h_prev = _adaln(prev_ref[0], g, sc, sh) * keep_prev
    h_next = _adaln(next_ref[0], g, sc, sh) * keep_next
    h_ext = jnp.concatenate([h_prev, h_mid, h_next], axis=0).astype(BF16)
    h_bf = h_mid.astype(BF16)
    for c in range(FF // FF_CHUNK):
        lo = c * FF_CHUNK
        gate_pre = jnp.dot(h_ext, wup_ref[:, lo:lo + FF_CHUNK], preferred_element_type=F32)
        up = jnp.dot(h_bf, wup_ref[:, FF + lo:FF + lo + FF_CHUNK], preferred_element_type=F32)
        convs = []
        for cb in range(FF_CHUNK // LANES):
            cols = slice(lo + cb * LANES, lo + (cb + 1) * LANES)
            gbuf_ref[cb] = gate_pre[:, cb * LANES:(cb + 1) * LANES]
            convs.append(gbuf_ref[cb, SUBLANES - 1:SUBLANES - 1 + tm, :] * cw_ref[0:1, cols]
                         + gbuf_ref[cb, SUBLANES:SUBLANES + tm, :] * cw_ref[1:2, cols]
                         + gbuf_ref[cb, SUBLANES + 1:SUBLANES + 1 + tm, :] * cw_ref[2:3, cols]
                         + cb_ref[:, cols])
        conv = jnp.concatenate(convs, axis=1)
        act_ref[:, lo:lo + FF_CHUNK] = ((conv / (1.0 + jnp.exp(-conv))) * up).astype(BF16)
    down = jnp.dot(act_ref[...], wdn_ref[...], preferred_element_type=F32)
    y_ref[0] = x + gate_ref[0] * down


def _ffn(x, g, sc, sh, gate, w_up, conv_w, conv_b, w_down):
    n, s, _ = x.shape
    tm = ROW_TILE
    hb = tm // SUBLANES
    last = s // SUBLANES - 1
    per_seq = lambda i, j: (i, 0, 0)
    const2 = lambda i, j: (0, 0)
    once = pl.Buffered(1)
    return pl.pallas_call(
        _ffn_kernel,
        grid=(n, s // tm),
        in_specs=[pl.BlockSpec((1, tm, D_MODEL), lambda i, j: (i, j, 0)),
                  pl.BlockSpec((1, SUBLANES, D_MODEL), lambda i, j: (i, jnp.maximum(j * hb - 1, 0), 0)),
                  pl.BlockSpec((1, SUBLANES, D_MODEL), lambda i, j: (i, jnp.minimum((j + 1) * hb, last), 0)),
                  pl.BlockSpec((1, D_MODEL), const2),
                  pl.BlockSpec((1, 1, D_MODEL), per_seq),
                  pl.BlockSpec((1, 1, D_MODEL), per_seq),
                  pl.BlockSpec((1, 1, D_MODEL), per_seq),
                  pl.BlockSpec((D_MODEL, 2 * FF), const2, pipeline_mode=once),
                  pl.BlockSpec((3, FF), const2),
                  pl.BlockSpec((1, FF), const2),
                  pl.BlockSpec((FF, D_MODEL), const2, pipeline_mode=once)],
        out_specs=pl.BlockSpec((1, tm, D_MODEL), lambda i, j: (i, j, 0)),
        out_shape=jax.ShapeDtypeStruct((n, s, D_MODEL), F32),
        scratch_shapes=[pltpu.VMEM((FF_CHUNK // LANES, tm + 2 * SUBLANES, LANES), F32),
                        pltpu.VMEM((tm, FF), BF16)],
        compiler_params=_cparams(2),
        name="ffn",
    )(x, x, x, g, sc, sh, gate, w_up, conv_w, conv_b.reshape(1, FF), w_down)


def _pair_cols(base, head_a, head_b):
    a = base + head_a * HEAD_DIM
    b = base + head_b * HEAD_DIM
    lo = np.arange(HALF_DIM)
    return np.concatenate([a + lo, b + lo, a + HALF_DIM + lo, b + HALF_DIM + lo])


def _w_in_perm():
    qa, ka, va = 0, D_A, D_A + D_A_KV
    qb = D_A + 2 * D_A_KV
    kb, vb = qb + D_B, qb + 2 * D_B
    ca = qb + 3 * D_B
    cols = [_pair_cols(qa, 0, 2), _pair_cols(qa, 1, 3), _pair_cols(ka, 0, 1)]
    cols += [_pair_cols(qb, 2 * j, 2 * j + 1) for j in range(3)]
    cols += [_pair_cols(kb, 2 * j, 2 * j + 1) for j in range(3)]
    cols += [np.arange(va, va + D_A_KV), np.arange(vb, vb + D_B), np.arange(ca, ca + 2 * C_CH)]
    return np.concatenate(cols)


def _w_out_perm():
    heads = [0, 2, 1, 3]
    rows = [np.arange(h * HEAD_DIM, (h + 1) * HEAD_DIM) for h in heads]
    return np.concatenate(rows + [np.arange(D_A, D_MIX)])


def _pair_gain(g, scale):
    return jnp.concatenate([g[:HALF_DIM], g[:HALF_DIM], g[HALF_DIM:], g[HALF_DIM:]]) * scale


def _head_mean_matrix():
    lane = np.arange(2 * LANES)
    head = (lane // LANES) * 2 + ((lane % HEAD_DIM) >= HALF_DIM)
    return jnp.asarray((head[:, None] == head[None, :]) / HEAD_DIM, dtype=BF16)


def _rope_tables(s):
    inv = 1.0 / (jnp.float32(ROPE_THETA) ** (jnp.arange(HALF_DIM, dtype=F32) / HALF_DIM))
    ang = jnp.arange(s, dtype=F32)[:, None] * inv[None, :]
    cos, sin = jnp.cos(ang), jnp.sin(ang)
    return (jnp.concatenate([cos, cos, cos, cos], axis=1),
            jnp.concatenate([-sin, -sin, sin, sin], axis=1))


def _layer_params(l, norm1_g, norm2_g, w_in, qn_a, kn_a, sink_a, qn_b, kn_b, w_out, w_up, w_down):
    q_scale = 1.0 / np.sqrt(HEAD_DIM)
    head_gain = jnp.concatenate(
        [_pair_gain(qn_a[l], q_scale)] * 2 + [_pair_gain(kn_a[l], 1.0)]
        + [_pair_gain(qn_b[l], q_scale)] * 3 + [_pair_gain(kn_b[l], 1.0)] * 3).reshape(1, N_NORM_COLS)
    sink = sink_a[l]
    sink_rows = jnp.stack([
        jnp.concatenate([jnp.full((HEAD_DIM,), sink[0]), jnp.full((HEAD_DIM,), sink[2])]),
        jnp.concatenate([jnp.full((HEAD_DIM,), sink[1]), jnp.full((HEAD_DIM,), sink[3])])])
    return dict(g1=norm1_g[l].reshape(1, D_MODEL), g2=norm2_g[l].reshape(1, D_MODEL),
                w_in=w_in[l][:, _w_in_perm()].astype(BF16), head_gain=head_gain, sink_rows=sink_rows,
                w_out=w_out[l][_w_out_perm(), :].astype(BF16),
                w_up=w_up[l].astype(BF16), w_down=w_down[l].astype(BF16))


def _trunk(x, mod, layers, tables, conv_c_w, conv_c_b, ln_c_g, ln_c_b, conv_f_w, conv_f_b):
    cosf, sinf, gmat, bias_a, bias_b = tables
    for l, p in enumerate(layers):
        sh1, sc1, g1, sh2, sc2, g2 = (mod[l, :, i] for i in range(6))
        qa, ka, va, qb, kb, vb, u = _inproj(x, p["g1"], sc1, sh1, p["w_in"], p["head_gain"],
                                            cosf, sinf, gmat)
        oa = _attn_a(qa, ka, va, bias_a, p["sink_rows"])
        ob = _attn_b(qb, kb, vb, bias_b)
        oc = _conv_c(u, conv_c_w[l], conv_c_b[l], ln_c_g[l], ln_c_b[l])
        x = _outproj(x, g1, oa, ob, oc, p["w_out"])
        x = _ffn(x, p["g2"], sc2, sh2, g2, p["w_up"], conv_f_w[l], conv_f_b[l], p["w_down"])
    return x


def kernel(x_prompt, x_sample, c_prompt, c_sample, norm1_g, norm2_g, w_mod, b_mod, w_in, qn_a, kn_a,
           sink_a, qn_b, kn_b, conv_c_w, conv_c_b, ln_c_g, ln_c_b, w_out, w_up, conv_f_w, conv_f_b,
           w_down):
    n_prompt, s, _ = x_prompt.shape
    c = jnp.concatenate([c_prompt, c_sample], axis=0)
    mod = _modulation(c, w_mod, b_mod).reshape(DEPTH, c.shape[0], 6, 1, D_MODEL)
    tables = _rope_tables(s) + (_head_mean_matrix(),
                                _band_bias(Q_TILE, Q_TILE + 2 * A_HALF_WINDOW, A_HALF_WINDOW),
                                _band_bias(Q_TILE, Q_TILE + 2 * B_HALF, B_HALF))
    layers = [_layer_params(l, norm1_g, norm2_g, w_in, qn_a, kn_a, sink_a, qn_b, kn_b, w_out, w_up, w_down)
              for l in range(DEPTH)]
    rest = (conv_c_w, conv_c_b, ln_c_g, ln_c_b, conv_f_w, conv_f_b)
    y_prompt = _trunk(x_prompt, mod[:, :n_prompt], layers, tables, *rest)
    y_sample = _trunk(x_sample, mod[:, n_prompt:], layers, tables, *rest)
    return y_prompt, y_sample
```

```python
import functools

import numpy as np
import jax
import jax.numpy as jnp
from jax import lax
from jax.experimental import pallas as pl
from jax.experimental.pallas import tpu as pltpu

D_MODEL = 1024
DEPTH = 2
HEAD_DIM = 64
HALF_DIM = HEAD_DIM // 2
LANES = 128
SUBLANES = 8
A_Q_HEADS = 4
A_KV_HEADS = 2
A_HALF_WINDOW = 128
B_HEADS = 6
B_PATTERNS = ((128, 1), (512, 4), (2048, 16))
B_HALF = 64
D_A = A_Q_HEADS * HEAD_DIM
D_A_KV = A_KV_HEADS * HEAD_DIM
D_B = B_HEADS * HEAD_DIM
C_CH = 384
C_CONV_WIDTH = 31
D_MIX = D_A + D_B + C_CH
IN_WIDTH = D_A + 2 * D_A_KV + 3 * D_B + 2 * C_CH
FF = 2816
FF_CHUNK = 2 * LANES
ROPE_THETA = 10000.0
EPS = 1e-6
NEG = -1e30

N_NORM_COLS = D_A + D_A_KV + 2 * D_B
N_V_COLS = D_A_KV + D_B
Q_TILE = 128
ROW_TILE = 512
COPY_ROWS = 256
BLOCK_UNROLL_A = 4
BLOCK_UNROLL_B = 8
VMEM_LIMIT = 56 * 1024 * 1024

BF16 = jnp.bfloat16
F32 = jnp.float32


def _cparams(n_axes, vmem=VMEM_LIMIT):
    return pltpu.CompilerParams(dimension_semantics=("parallel",) * n_axes,
                                vmem_limit_bytes=vmem)


def _lane_iota(shape):
    return lax.broadcasted_iota(jnp.int32, shape, len(shape) - 1)


def _mod_kernel(c_ref, w_ref, b_ref, o_ref):
    c = c_ref[...]
    a = c / (1.0 + jnp.exp(-c))
    w = w_ref[0]
    a_hi = a.astype(BF16)
    a_lo = (a - a_hi.astype(F32)).astype(BF16)
    w_hi = w.astype(BF16)
    w_lo = (w - w_hi.astype(F32)).astype(BF16)
    acc = jnp.dot(a_hi, w_hi, preferred_element_type=F32)
    acc += jnp.dot(a_lo, w_hi, preferred_element_type=F32)
    acc += jnp.dot(a_hi, w_lo, preferred_element_type=F32)
    o_ref[0] = acc + b_ref[0]


def _modulation(c, w_mod, b_mod):
    n = c.shape[0]
    tn = 1536
    return pl.pallas_call(
        _mod_kernel,
        grid=(DEPTH, 6 * D_MODEL // tn),
        in_specs=[pl.BlockSpec((n, D_MODEL), lambda l, j: (0, 0)),
                  pl.BlockSpec((1, D_MODEL, tn), lambda l, j: (l, 0, j)),
                  pl.BlockSpec((1, 1, tn), lambda l, j: (l, 0, j))],
        out_specs=pl.BlockSpec((1, n, tn), lambda l, j: (l, 0, j)),
        out_shape=jax.ShapeDtypeStruct((DEPTH, n, 6 * D_MODEL), F32),
        compiler_params=_cparams(2),
        name="modulation",
    )(c, w_mod, b_mod.reshape(DEPTH, 1, 6 * D_MODEL))


def _adaln(x, g, sc, sh):
    ms = jnp.mean(x * x, axis=-1, keepdims=True)
    return (x * lax.rsqrt(ms + EPS) * g) * (1.0 + sc) + sh


def _inproj_kernel(x_ref, g_ref, sc_ref, sh_ref, w_ref, hg_ref, cos_ref, sin_ref, gm_ref,
                   qa_ref, ka_ref, va_ref, qb_ref, kb_ref, vb_ref, u_ref):
    h = _adaln(x_ref[0], g_ref[...], sc_ref[0], sh_ref[0]).astype(BF16)
    proj = jnp.dot(h, w_ref[...], preferred_element_type=F32)
    cosf = cos_ref[...]
    sinf = sin_ref[...]
    gm = gm_ref[...]

    def norm_rope(c0, width):
        p = proj[:, c0:c0 + width]
        sq = (p * p).astype(BF16)
        ms = jnp.dot(sq, gm[:width, :width], preferred_element_type=F32)
        y = p * lax.rsqrt(ms + EPS) * hg_ref[:, c0:c0 + width]
        outs = []
        for j in range(width // LANES):
            yb = y[:, j * LANES:(j + 1) * LANES]
            outs.append(yb * cosf + pltpu.roll(yb, HEAD_DIM, 1) * sinf)
        return outs

    blocks = []
    for c0 in range(0, N_NORM_COLS, 2 * LANES):
        blocks += norm_rope(c0, min(2 * LANES, N_NORM_COLS - c0))
    qa_ref[0, 0] = blocks[0].astype(BF16)
    qa_ref[0, 1] = blocks[1].astype(BF16)
    ka_ref[0] = blocks[2].astype(BF16)
    for j in range(3):
        qb_ref[0, j] = blocks[3 + j]
        kb_ref[0, j] = blocks[6 + j]
    v0 = N_NORM_COLS
    va_ref[0] = proj[:, v0:v0 + LANES].astype(BF16)
    for j in range(3):
        vb_ref[0, j] = proj[:, v0 + (1 + j) * LANES:v0 + (2 + j) * LANES]
    c0 = N_NORM_COLS + N_V_COLS
    ca = proj[:, c0:c0 + C_CH]
    cg = proj[:, c0 + C_CH:c0 + 2 * C_CH]
    u_ref[0] = (ca / (1.0 + jnp.exp(-cg))).astype(BF16)


def _inproj(x, g, sc, sh, w_in, head_gain, cosf, sinf, gmat):
    n, s, _ = x.shape
    tm = ROW_TILE
    row = lambda i, j: (i, j, 0)
    hp_row = lambda i, j: (i, 0, j, 0)
    const2 = lambda i, j: (0, 0)
    per_seq = lambda i, j: (i, 0, 0)
    return pl.pallas_call(
        _inproj_kernel,
        grid=(n, s // tm),
        in_specs=[pl.BlockSpec((1, tm, D_MODEL), row),
                  pl.BlockSpec((1, D_MODEL), const2),
                  pl.BlockSpec((1, 1, D_MODEL), per_seq),
                  pl.BlockSpec((1, 1, D_MODEL), per_seq),
                  pl.BlockSpec((D_MODEL, IN_WIDTH), const2),
                  pl.BlockSpec((1, N_NORM_COLS), const2),
                  pl.BlockSpec((tm, LANES), lambda i, j: (j, 0)),
                  pl.BlockSpec((tm, LANES), lambda i, j: (j, 0)),
                  pl.BlockSpec((2 * LANES, 2 * LANES), const2)],
        out_specs=[pl.BlockSpec((1, 2, tm, LANES), hp_row),
                   pl.BlockSpec((1, tm, LANES), row),
                   pl.BlockSpec((1, tm, LANES), row),
                   pl.BlockSpec((1, 3, tm, LANES), hp_row),
                   pl.BlockSpec((1, 3, tm, LANES), hp_row),
                   pl.BlockSpec((1, 3, tm, LANES), hp_row),
                   pl.BlockSpec((1, tm, C_CH), row)],
        out_shape=[jax.ShapeDtypeStruct((n, 2, s, LANES), BF16),
                   jax.ShapeDtypeStruct((n, s, LANES), BF16),
                   jax.ShapeDtypeStruct((n, s, LANES), BF16),
                   jax.ShapeDtypeStruct((n, 3, s, LANES), F32),
                   jax.ShapeDtypeStruct((n, 3, s, LANES), F32),
                   jax.ShapeDtypeStruct((n, 3, s, LANES), F32),
                   jax.ShapeDtypeStruct((n, s, C_CH), BF16)],
        compiler_params=_cparams(2),
        name="inproj",
    )(x, g, sc, sh, w_in, head_gain, cosf, sinf, gmat)


def _head_masks():
    lane = _lane_iota((1, LANES))
    qk_a = jnp.where((lane % HEAD_DIM) < HALF_DIM, 1.0, 0.0).astype(BF16)
    v_a = lane < HEAD_DIM
    return qk_a, jnp.where(v_a, 1.0, 0.0).astype(BF16), v_a


def _split_heads(q, qk_a):
    return q * qk_a, q * (1.0 - qk_a).astype(BF16)


def _aug_values(v, v_m):
    v_n = (1.0 - v_m).astype(BF16)
    aug_a = jnp.concatenate([v * v_m, jnp.broadcast_to(v_m, v.shape)], axis=1)
    aug_b = jnp.concatenate([v * v_n, jnp.broadcast_to(v_n, v.shape)], axis=1)
    return aug_a, aug_b


def _scores(lhs, k):
    return lax.dot_general(lhs, k, (((1,), (1,)), ((), ())), preferred_element_type=F32)


def _band_window(t, n_tiles, half, k_tile, length):
    k0 = jnp.clip(t * Q_TILE - half, 0, length - k_tile)
    variant = jnp.where(t == 0, 0, jnp.where(t == n_tiles - 1, 2, 1))
    return k0, variant


def _attn_b_kernel(q_ref, k_ref, v_ref, bias_ref, o_ref,
                   qd_ref, kd_ref, vd_ref, osub_ref, lsub_ref, onat_ref, lnat_ref, *, s):
    qk_a, v_m, v_a = _head_masks()
    kt = Q_TILE + 2 * B_HALF
    n_blocks = s // Q_TILE

    for pi, (_, dil) in enumerate(B_PATTERNS):
        ls = s // dil
        nt = ls // Q_TILE
        rows = min(COPY_ROWS, ls)
        for r in range(dil):
            for c0 in range(0, ls, rows):
                src = pl.ds(r + c0 * dil, rows, stride=dil) if dil > 1 else pl.ds(c0, rows)
                dst = pl.ds(r * ls + c0, rows)
                qd_ref[dst, :] = q_ref[0, 0, src, :].astype(BF16)
                kd_ref[dst, :] = k_ref[0, 0, src, :].astype(BF16)
                vd_ref[dst, :] = v_ref[0, 0, src, :].astype(BF16)

        o_dst = onat_ref.at[pi] if dil == 1 else osub_ref
        l_dst = lnat_ref.at[pi] if dil == 1 else lsub_ref

        def block(b, carry, ls=ls, nt=nt, o_dst=o_dst, l_dst=l_dst):
            t = b % nt
            kl, variant = _band_window(t, nt, B_HALF, kt, ls)
            q0 = pl.multiple_of(b * Q_TILE, Q_TILE)
            k0 = pl.multiple_of((b // nt) * ls + kl, B_HALF)
            q = qd_ref[pl.ds(q0, Q_TILE), :]
            k = kd_ref[pl.ds(k0, kt), :]
            v = vd_ref[pl.ds(k0, kt), :]
            qa, qb = _split_heads(q, qk_a)
            bias = bias_ref[variant]
            sa = _scores(qa, k) + bias
            sb = _scores(qb, k) + bias
            ma = jnp.max(sa, axis=1, keepdims=True)
            mb = jnp.max(sb, axis=1, keepdims=True)
            pa = jnp.exp(sa - ma).astype(BF16)
            pb = jnp.exp(sb - mb).astype(BF16)
            aug_a, aug_b = _aug_values(v, v_m)
            res = (jnp.dot(pa, aug_a, preferred_element_type=F32)
                   + jnp.dot(pb, aug_b, preferred_element_type=F32))
            den = res[:, LANES:]
            o_dst[pl.ds(q0, Q_TILE), :] = res[:, :LANES] / den
            l_dst[pl.ds(q0, Q_TILE), :] = jnp.where(v_a, ma, mb) + jnp.log(den)
            return carry
        lax.fori_loop(0, n_blocks, block, 0, unroll=BLOCK_UNROLL_B)

        if dil > 1:
            for r in range(dil):
                for c0 in range(0, ls, rows):
                    src = pl.ds(r * ls + c0, rows)
                    dst = pl.ds(r + c0 * dil, rows, stride=dil)
                    onat_ref[pi, dst, :] = osub_ref[src, :]
                    lnat_ref[pi, dst, :] = lsub_ref[src, :]

    def merge(c, carry):
        r0 = pl.multiple_of(c * Q_TILE, Q_TILE)
        sl = pl.ds(r0, Q_TILE)
        l1, l2, l3 = lnat_ref[0, sl, :], lnat_ref[1, sl, :], lnat_ref[2, sl, :]
        m = jnp.maximum(jnp.maximum(l1, l2), l3)
        e1, e2, e3 = jnp.exp(l1 - m), jnp.exp(l2 - m), jnp.exp(l3 - m)
        mixed = e1 * onat_ref[0, sl, :] + e2 * onat_ref[1, sl, :] + e3 * onat_ref[2, sl, :]
        o_ref[0, 0, sl, :] = (mixed / (e1 + e2 + e3)).astype(BF16)
        return carry
    lax.fori_loop(0, n_blocks, merge, 0, unroll=2)


def _attn_b(q, k, v, bias):
    n, _, s, _ = q.shape
    spec = pl.BlockSpec((1, 1, s, LANES), lambda i, j: (i, j, 0, 0))
    return pl.pallas_call(
        functools.partial(_attn_b_kernel, s=s),
        grid=(n, 3),
        in_specs=[spec, spec, spec,
                  pl.BlockSpec((3, Q_TILE, Q_TILE + 2 * B_HALF), lambda i, j: (0, 0, 0))],
        out_specs=spec,
        out_shape=jax.ShapeDtypeStruct((n, 3, s, LANES), BF16),
        scratch_shapes=[pltpu.VMEM((s, LANES), BF16), pltpu.VMEM((s, LANES), BF16),
                        pltpu.VMEM((s, LANES), BF16),
                        pltpu.VMEM((s, LANES), F32), pltpu.VMEM((s, LANES), F32),
                        pltpu.VMEM((3, s, LANES), F32), pltpu.VMEM((3, s, LANES), F32)],
        compiler_params=_cparams(2),
        name="attn_b",
    )(q, k, v, bias)


def _attn_a_kernel(q_ref, k_ref, v_ref, bias_ref, sink_ref, o_ref, *, s):
    qk_a, v_m, v_a = _head_masks()
    n_tiles = s // Q_TILE
    kt = Q_TILE + 2 * A_HALF_WINDOW

    def block(t, carry):
        kl, variant = _band_window(t, n_tiles, A_HALF_WINDOW, kt, s)
        q0 = pl.multiple_of(t * Q_TILE, Q_TILE)
        k0 = pl.multiple_of(kl, A_HALF_WINDOW)
        k = k_ref[0, pl.ds(k0, kt), :]
        v = v_ref[0, pl.ds(k0, kt), :]
        aug_a, aug_b = _aug_values(v, v_m)
        bias = bias_ref[variant]
        for c in range(2):
            q = q_ref[0, c, pl.ds(q0, Q_TILE), :]
            qa, qb = _split_heads(q, qk_a)
            sink = sink_ref[c:c + 1, :]
            sa = _scores(qa, k) + bias
            sb = _scores(qb, k) + bias
            ma = jnp.maximum(jnp.max(sa, axis=1, keepdims=True), sink[:, 0:1])
            mb = jnp.maximum(jnp.max(sb, axis=1, keepdims=True), sink[:, LANES - 1:LANES])
            pa = jnp.exp(sa - ma).astype(BF16)
            pb = jnp.exp(sb - mb).astype(BF16)
            res = (jnp.dot(pa, aug_a, preferred_element_type=F32)
                   + jnp.dot(pb, aug_b, preferred_element_type=F32))
            den = res[:, LANES:] + jnp.exp(sink - jnp.where(v_a, ma, mb))
            o_ref[0, c, pl.ds(q0, Q_TILE), :] = (res[:, :LANES] / den).astype(BF16)
        return carry
    lax.fori_loop(0, n_tiles, block, 0, unroll=BLOCK_UNROLL_A)


def _attn_a(q, k, v, bias, sink_rows):
    n, _, s, _ = q.shape
    kt = Q_TILE + 2 * A_HALF_WINDOW
    return pl.pallas_call(
        functools.partial(_attn_a_kernel, s=s),
        grid=(n,),
        in_specs=[pl.BlockSpec((1, 2, s, LANES), lambda i: (i, 0, 0, 0)),
                  pl.BlockSpec((1, s, LANES), lambda i: (i, 0, 0)),
                  pl.BlockSpec((1, s, LANES), lambda i: (i, 0, 0)),
                  pl.BlockSpec((3, Q_TILE, kt), lambda i: (0, 0, 0)),
                  pl.BlockSpec((2, LANES), lambda i: (0, 0))],
        out_specs=pl.BlockSpec((1, 2, s, LANES), lambda i: (i, 0, 0, 0)),
        out_shape=jax.ShapeDtypeStruct((n, 2, s, LANES), BF16),
        compiler_params=_cparams(1),
        name="attn_a",
    )(q, k, v, bias, sink_rows)


def _band_bias(q_tile, k_tile, half):
    i = np.arange(q_tile)[:, None]
    j = np.arange(k_tile)[None, :]
    out = []
    for delta in (0, -half, -2 * half):
        out.append(np.where(np.abs(j + delta - i) <= half, 0.0, NEG))
    return jnp.asarray(np.stack(out), dtype=F32)


C_BLOCKS = C_CH // LANES


HALO = 16
U_HALO = 32
MIX_CONV_ROWS = 32


def _mixffn_kernel(x_ref, xp_ref, xn_ref, oa_ref, oap_ref, oan_ref, ob_ref, obp_ref, obn_ref,
                   u_ref, up_ref, un_ref, ccw_ref, ccb_ref, lng_ref, lnb_ref, gate1_ref, wout_ref,
                   g2_ref, sc_ref, sh_ref, gate2_ref, wup_ref, fcw_ref, fcb_ref, wdn_ref,
                   y_ref, ubuf_ref, mix_ref, gbuf_ref, act_ref):
    tm = x_ref.shape[1]
    ext = tm + 2 * HALO
    j = pl.program_id(1)
    keep_prev = jnp.where(j > 0, 1.0, 0.0)
    keep_next = jnp.where(j < pl.num_programs(1) - 1, 1.0, 0.0)

    for cb in range(C_BLOCKS):
        cols = slice(cb * LANES, (cb + 1) * LANES)
        ubuf_ref[cb, 0:U_HALO, :] = up_ref[0, :, cols].astype(F32) * keep_prev
        ubuf_ref[cb, U_HALO:U_HALO + tm, :] = u_ref[0, :, cols].astype(F32)
        ubuf_ref[cb, U_HALO + tm:2 * U_HALO + tm, :] = un_ref[0, :, cols].astype(F32) * keep_next
    first_tap = U_HALO - HALO - (C_CONV_WIDTH - 1) // 2
    for e0 in range(0, ext, MIX_CONV_ROWS):
        accs = []
        for cb in range(C_BLOCKS):
            cols = slice(cb * LANES, (cb + 1) * LANES)
            acc = jnp.broadcast_to(ccb_ref[:, cols], (MIX_CONV_ROWS, LANES))
            for kk in range(C_CONV_WIDTH):
                lo = e0 + first_tap + kk
                acc = acc + ubuf_ref[cb, lo:lo + MIX_CONV_ROWS, :] * ccw_ref[kk:kk + 1, cols]
            accs.append(acc)
        acc = jnp.concatenate(accs, axis=1)
        mu = jnp.mean(acc, axis=-1, keepdims=True)
        cen = acc - mu
        var = jnp.mean(cen * cen, axis=-1, keepdims=True)
        yc = cen * lax.rsqrt(var + EPS) * lng_ref[...] + lnb_ref[...]
        mix_ref[e0:e0 + MIX_CONV_ROWS, D_A + D_B:] = (yc / (1.0 + jnp.exp(-yc))).astype(BF16)

    for c in range(2):
        cols = slice(c * LANES, (c + 1) * LANES)
        mix_ref[0:HALO, cols] = oap_ref[0, c]
        mix_ref[HALO:HALO + tm, cols] = oa_ref[0, c]
        mix_ref[HALO + tm:ext, cols] = oan_ref[0, c]
    for c in range(3):
        cols = slice(D_A + c * LANES, D_A + (c + 1) * LANES)
        mix_ref[0:HALO, cols] = obp_ref[0, c]
        mix_ref[HALO:HALO + tm, cols] = ob_ref[0, c]
        mix_ref[HALO + tm:ext, cols] = obn_ref[0, c]
    x_ext = jnp.concatenate([xp_ref[0], x_ref[0], xn_ref[0]], axis=0)
    x_new = x_ext + gate1_ref[0] * jnp.dot(mix_ref[...], wout_ref[...], preferred_element_type=F32)

    row = lax.broadcasted_iota(jnp.int32, (ext, 1), 0)
    keep = jnp.where(row < HALO, keep_prev, jnp.where(row >= HALO + tm, keep_next, 1.0))
    h_ext = _adaln(x_new, g2_ref[...], sc_ref[0], sh_ref[0]) * keep
    lo_row = HALO - SUBLANES
    h_gate = h_ext[lo_row:lo_row + tm + 2 * SUBLANES].astype(BF16)
    h_mid = h_ext[HALO:HALO + tm].astype(BF16)
    for c in range(FF // FF_CHUNK):
        lo = c * FF_CHUNK
        gate_pre = jnp.dot(h_gate, wup_ref[:, lo:lo + FF_CHUNK], preferred_element_type=F32)
        up = jnp.dot(h_mid, wup_ref[:, FF + lo:FF + lo + FF_CHUNK], preferred_element_type=F32)
        convs = []
        for cb in range(FF_CHUNK // LANES):
            cols = slice(lo + cb * LANES, lo + (cb + 1) * LANES)
            gbuf_ref[cb] = gate_pre[:, cb * LANES:(cb + 1) * LANES]
            convs.append(gbuf_ref[cb, SUBLANES - 1:SUBLANES - 1 + tm, :] * fcw_ref[0:1, cols]
                         + gbuf_ref[cb, SUBLANES:SUBLANES + tm, :] * fcw_ref[1:2, cols]
                         + gbuf_ref[cb, SUBLANES + 1:SUBLANES + 1 + tm, :] * fcw_ref[2:3, cols]
                         + fcb_ref[:, cols])
        conv = jnp.concatenate(convs, axis=1)
        act_ref[:, lo:lo + FF_CHUNK] = ((conv / (1.0 + jnp.exp(-conv))) * up).astype(BF16)
    down = jnp.dot(act_ref[...], wdn_ref[...], preferred_element_type=F32)
    y_ref[0] = x_new[HALO:HALO + tm] + gate2_ref[0] * down


def _mixffn(x, oa, ob, u, conv_c_w, conv_c_b, ln_g, ln_b, gate1, w_out,
            g2, sc2, sh2, gate2, w_up, conv_f_w, conv_f_b, w_down):
    n, s, _ = x.shape
    tm = ROW_TILE
    ext = tm + 2 * HALO

    def halo_specs(block, rows, lead):
        per_tile = tm // rows
        last = s // rows - 1
        zeros = (0,) * (len(block) - lead - 2)
        prev = lambda i, j: (i,) + zeros + (jnp.maximum(j * per_tile - 1, 0), 0)
        nxt = lambda i, j: (i,) + zeros + (jnp.minimum((j + 1) * per_tile, last), 0)
        return pl.BlockSpec(block, prev), pl.BlockSpec(block, nxt)

    row = lambda i, j: (i, j, 0)
    hp_row = lambda i, j: (i, 0, j, 0)
    per_seq = lambda i, j: (i, 0, 0)
    const2 = lambda i, j: (0, 0)
    once = pl.Buffered(1)
    vec_c = pl.BlockSpec((1, C_CH), const2)
    vec_d = pl.BlockSpec((1, 1, D_MODEL), per_seq)
    return pl.pallas_call(
        _mixffn_kernel,
        grid=(n, s // tm),
        in_specs=[pl.BlockSpec((1, tm, D_MODEL), row), *halo_specs((1, HALO, D_MODEL), HALO, 1),
                  pl.BlockSpec((1, 2, tm, LANES), hp_row), *halo_specs((1, 2, HALO, LANES), HALO, 1),
                  pl.BlockSpec((1, 3, tm, LANES), hp_row), *halo_specs((1, 3, HALO, LANES), HALO, 1),
                  pl.BlockSpec((1, tm, C_CH), row), *halo_specs((1, U_HALO, C_CH), U_HALO, 1),
                  pl.BlockSpec((C_CONV_WIDTH, C_CH), const2), vec_c, vec_c, vec_c,
                  vec_d, pl.BlockSpec((D_MIX, D_MODEL), const2, pipeline_mode=once),
                  pl.BlockSpec((1, D_MODEL), const2), vec_d, vec_d, vec_d,
                  pl.BlockSpec((D_MODEL, 2 * FF), const2, pipeline_mode=once),
                  pl.BlockSpec((3, FF), const2),
                  pl.BlockSpec((1, FF), const2),
                  pl.BlockSpec((FF, D_MODEL), const2, pipeline_mode=once)],
        out_specs=pl.BlockSpec((1, tm, D_MODEL), row),
        out_shape=jax.ShapeDtypeStruct((n, s, D_MODEL), F32),
        scratch_shapes=[pltpu.VMEM((C_BLOCKS, tm + 2 * U_HALO, LANES), F32),
                        pltpu.VMEM((ext, D_MIX), BF16),
                        pltpu.VMEM((FF_CHUNK // LANES, tm + 2 * SUBLANES, LANES), F32),
                        pltpu.VMEM((tm, FF), BF16)],
        compiler_params=_cparams(2),
        name="mixffn",
    )(x, x, x, oa, oa, oa, ob, ob, ob, u, u, u,
      conv_c_w, conv_c_b.reshape(1, C_CH), ln_g.reshape(1, C_CH), ln_b.reshape(1, C_CH), gate1, w_out,
      g2, sc2, sh2, gate2, w_up, conv_f_w, conv_f_b.reshape(1, FF), w_down)


def _pair_cols(base, head_a, head_b):
    a = base + head_a * HEAD_DIM
    b = base + head_b * HEAD_DIM
    lo = np.arange(HALF_DIM)
    return np.concatenate([a + lo, b + lo, a + HALF_DIM + lo, b + HALF_DIM + lo])


def _w_in_perm():
    qa, ka, va = 0, D_A, D_A + D_A_KV
    qb = D_A + 2 * D_A_KV
    kb, vb = qb + D_B, qb + 2 * D_B
    ca = qb + 3 * D_B
    cols = [_pair_cols(qa, 0, 2), _pair_cols(qa, 1, 3), _pair_cols(ka, 0, 1)]
    cols += [_pair_cols(qb, 2 * j, 2 * j + 1) for j in range(3)]
    cols += [_pair_cols(kb, 2 * j, 2 * j + 1) for j in range(3)]
    cols += [np.arange(va, va + D_A_KV), np.arange(vb, vb + D_B), np.arange(ca, ca + 2 * C_CH)]
    return np.concatenate(cols)


def _w_out_perm():
    heads = [0, 2, 1, 3]
    rows = [np.arange(h * HEAD_DIM, (h + 1) * HEAD_DIM) for h in heads]
    return np.concatenate(rows + [np.arange(D_A, D_MIX)])


def _pair_gain(g, scale):
    return jnp.concatenate([g[:HALF_DIM], g[:HALF_DIM], g[HALF_DIM:], g[HALF_DIM:]]) * scale


def _head_mean_matrix():
    lane = np.arange(2 * LANES)
    head = (lane // LANES) * 2 + ((lane % HEAD_DIM) >= HALF_DIM)
    return jnp.asarray((head[:, None] == head[None, :]) / HEAD_DIM, dtype=BF16)


def _rope_tables(s):
    inv = 1.0 / (jnp.float32(ROPE_THETA) ** (jnp.arange(HALF_DIM, dtype=F32) / HALF_DIM))
    ang = jnp.arange(s, dtype=F32)[:, None] * inv[None, :]
    cos, sin = jnp.cos(ang), jnp.sin(ang)
    return (jnp.concatenate([cos, cos, cos, cos], axis=1),
            jnp.concatenate([-sin, -sin, sin, sin], axis=1))


def _layer_params(l, norm1_g, norm2_g, w_in, qn_a, kn_a, sink_a, qn_b, kn_b, w_out, w_up, w_down):
    q_scale = 1.0 / np.sqrt(HEAD_DIM)
    head_gain = jnp.concatenate(
        [_pair_gain(qn_a[l], q_scale)] * 2 + [_pair_gain(kn_a[l], 1.0)]
        + [_pair_gain(qn_b[l], q_scale)] * 3 + [_pair_gain(kn_b[l], 1.0)] * 3).reshape(1, N_NORM_COLS)
    sink = sink_a[l]
    sink_rows = jnp.stack([
        jnp.concatenate([jnp.full((HEAD_DIM,), sink[0]), jnp.full((HEAD_DIM,), sink[2])]),
        jnp.concatenate([jnp.full((HEAD_DIM,), sink[1]), jnp.full((HEAD_DIM,), sink[3])])])
    return dict(g1=norm1_g[l].reshape(1, D_MODEL), g2=norm2_g[l].reshape(1, D_MODEL),
                w_in=w_in[l][:, _w_in_perm()].astype(BF16), head_gain=head_gain, sink_rows=sink_rows,
                w_out=w_out[l][_w_out_perm(), :].astype(BF16),
                w_up=w_up[l].astype(BF16), w_down=w_down[l].astype(BF16))


def _trunk(x, mod, layers, tables, conv_c_w, conv_c_b, ln_c_g, ln_c_b, conv_f_w, conv_f_b):
    cosf, sinf, gmat, bias_a, bias_b = tables
    for l, p in enumerate(layers):
        sh1, sc1, g1, sh2, sc2, g2 = (mod[l, :, i] for i in range(6))
        qa, ka, va, qb, kb, vb, u = _inproj(x, p["g1"], sc1, sh1, p["w_in"], p["head_gain"],
                                            cosf, sinf, gmat)
        oa = _attn_a(qa, ka, va, bias_a, p["sink_rows"])
        ob = _attn_b(qb, kb, vb, bias_b)
        x = _mixffn(x, oa, ob, u, conv_c_w[l], conv_c_b[l], ln_c_g[l], ln_c_b[l], g1, p["w_out"],
                    p["g2"], sc2, sh2, g2, p["w_up"], conv_f_w[l], conv_f_b[l], p["w_down"])
    return x


def kernel(x_prompt, x_sample, c_prompt, c_sample, norm1_g, norm2_g, w_mod, b_mod, w_in, qn_a, kn_a,
           sink_a, qn_b, kn_b, conv_c_w, conv_c_b, ln_c_g, ln_c_b, w_out, w_up, conv_f_w, conv_f_b,
           w_down):
    n_prompt, s, _ = x_prompt.shape
    c = jnp.concatenate([c_prompt, c_sample], axis=0)
    mod = _modulation(c, w_mod, b_mod).reshape(DEPTH, c.shape[0], 6, 1, D_MODEL)
    tables = _rope_tables(s) + (_head_mean_matrix(),
                                _band_bias(Q_TILE, Q_TILE + 2 * A_HALF_WINDOW, A_HALF_WINDOW),
                                _band_bias(Q_TILE, Q_TILE + 2 * B_HALF, B_HALF))
    layers = [_layer_params(l, norm1_g, norm2_g, w_in, qn_a, kn_a, sink_a, qn_b, kn_b, w_out, w_up, w_down)
              for l in range(DEPTH)]
    rest = (conv_c_w, conv_c_b, ln_c_g, ln_c_b, conv_f_w, conv_f_b)
    y_prompt = _trunk(x_prompt, mod[:, :n_prompt], layers, tables, *rest)
    y_sample = _trunk(x_sample, mod[:, n_prompt:], layers, tables, *rest)
    return y_prompt, y_sample
```
